```python
import math
import jax, jax.numpy as jnp
from jax import lax
import numpy as np

D_MODEL = 2048
BATCH = 4
SEQ = 8192
DEPTH = 1

CTX_LEN = 256
GRID_W = 64
EPS = 1e-6

D_FF = 5632

A_WIDTH = D_MODEL
A_GROUPS = 8
A_CHUNK = 128

B_INNER = 2 * D_MODEL
B_HEADDIM = 64
B_HEADS = B_INNER // B_HEADDIM
B_GROUPS = 8
B_STATE = 128
B_CONV = 5
SSD_CHUNK = 128

XB_W = B_INNER + B_GROUPS * B_STATE
XBC_W = XB_W + B_GROUPS * B_STATE
OFF_U = 0
OFF_V = OFF_U + A_WIDTH
OFF_Z = OFF_V + A_WIDTH
OFF_XB = OFF_Z + B_INNER
OFF_C = OFF_XB + XB_W
OFF_DT = OFF_C + B_GROUPS * B_STATE
OFF_GATE = OFF_DT + 2 * B_HEADS
IN_W = OFF_GATE + 2 * D_MODEL

kernel_name = 'hybrid_gmlp_ssd_prefix_block'


def rmsnorm(x, w):
    xf = x.astype(jnp.float32)
    y = xf * lax.rsqrt(jnp.mean(xf * xf, axis=-1, keepdims=True) + EPS)
    return (y * w.astype(jnp.float32)).astype(x.dtype)


def layernorm(x, w, b):
    xf = x.astype(jnp.float32)
    mu = jnp.mean(xf, axis=-1, keepdims=True)
    var = jnp.mean(jnp.square(xf - mu), axis=-1, keepdims=True)
    return ((xf - mu) * lax.rsqrt(var + EPS) * w.astype(jnp.float32) + b.astype(jnp.float32)).astype(x.dtype)


def modulate(h, shift, scale):
    return h * (1 + scale) + shift


def adaln(cvec, w_mod, b_mod):
    m = jax.nn.silu(cvec) @ w_mod + b_mod
    if m.ndim == 2:
        m = m[:, None, :]
    return jnp.split(m, 9, axis=-1)


def swiglu_half_step(x, mod3, g, w_gate, w_up, w_down):
    shift, scale, gate = mod3
    h = modulate(rmsnorm(x, g), shift, scale)
    return x + 0.5 * gate * ((jax.nn.silu(h @ w_gate) * (h @ w_up)) @ w_down)


def dwconv_centred(u, w, b, n_seg):
    bsz, length, ch = u.shape
    seg = length // n_seg
    pad = (w.shape[0] - 1) // 2
    y = lax.conv_general_dilated(u.reshape(bsz * n_seg, seg, ch), w[:, None, :].astype(u.dtype),
                                 window_strides=(1,), padding=[(pad, pad)],
                                 dimension_numbers=('NWC', 'WIO', 'NWC'), feature_group_count=ch)
    return y.reshape(bsz, length, ch) + b


def chunk_gmlp(u, v, ln_w, ln_b, w_s, b_s):
    bsz, length, width = u.shape
    nc = length // A_CHUNK
    vn = layernorm(v, ln_w, ln_b).reshape(bsz, nc, A_CHUNK, A_GROUPS, width // A_GROUPS)
    s = jnp.einsum('gij,bcjgd->bcigd', w_s, vn) + b_s.T[:, :, None]
    return u * s.reshape(bsz, length, width)


def segsum_exp(a_cs):
    q = a_cs.shape[-1]
    causal = jnp.tril(jnp.ones((q, q), dtype=bool))
    diff = a_cs[..., :, None] - a_cs[..., None, :]
    return jnp.exp(jnp.where(causal, diff, -jnp.inf))


def ssd_single(xh, dt, bm, cm, h0, a_h):
    length, nh, hp = xh.shape
    ng, ns = bm.shape[1], bm.shape[2]
    hpg = nh // ng
    q = SSD_CHUNK
    nc = length // q
    a = (dt * a_h).astype(jnp.float32).reshape(nc, q, ng, hpg).transpose(0, 2, 3, 1)
    a_cs = jnp.cumsum(a, axis=-1)
    xc = (xh * dt[..., None]).reshape(nc, q, ng, hpg, hp)
    bc = bm.reshape(nc, q, ng, ns)
    cc = cm.reshape(nc, q, ng, ns)
    cb = jnp.einsum('cign,cjgn->cgij', cc, bc)
    y_diag = jnp.einsum('cgij,cghij,cjghp->cighp', cb, segsum_exp(a_cs), xc)
    decay_to_end = jnp.exp(a_cs[..., -1:] - a_cs)
    states = jnp.einsum('cjgn,cghj,cjghp->cghpn', bc, decay_to_end, xc).astype(jnp.float32)
    chunk_decay = jnp.exp(a_cs[..., -1])

    def step(h, inp):
        dec, st = inp
        return dec[..., None, None] * h + st, h

    _, h_enter = lax.scan(step, h0.astype(jnp.float32).reshape(ng, hpg, hp, ns), (chunk_decay, states))
    y_off = jnp.einsum('cign,cghi,cghpn->cighp', cc, jnp.exp(a_cs), h_enter)
    return (y_diag + y_off).reshape(length, nh, hp).astype(xh.dtype)


def ssd_final_state(xh, dt, a_h, bm):
    bsz, length, nh, hp = xh.shape
    a_cs = jnp.cumsum((dt * a_h).astype(jnp.float32), axis=1)
    decay = jnp.exp(a_cs[:, -1:] - a_cs)
    xw = (xh * (dt * decay)[..., None]).reshape(bsz, length, B_GROUPS, nh // B_GROUPS, hp)
    h = jnp.einsum('blghp,blgn->bghpn', xw, bm)
    return h.reshape(bsz, nh, hp, B_STATE)


def flip_seq(t):
    return jnp.flip(t, axis=1)


def bidir_ssd(xh, bm, cm, dt_f, dt_b, a_neg, h0_f, h0_b):
    def run(xs, dts, a_h, bs, cs, h0):
        return lax.map(lambda t: ssd_single(t[0], t[1], t[2], t[3], t[4], a_h), (xs, dts, bs, cs, h0))
    y_f = run(xh, dt_f, a_neg[0], bm, cm, h0_f)
    y_b = flip_seq(run(flip_seq(xh), flip_seq(dt_b), a_neg[1], flip_seq(bm), flip_seq(cm), h0_b))
    return y_f, y_b


def ssm_xb(p_xb, p_dt, conv_w, conv_b, dt_bias, n_seg):
    xb = jax.nn.silu(dwconv_centred(p_xb, conv_w[:, :XB_W], conv_b[:XB_W], n_seg))
    bsz, length, _ = xb.shape
    xh = xb[..., :B_INNER].reshape(bsz, length, B_HEADS, B_HEADDIM)
    bm = xb[..., B_INNER:].reshape(bsz, length, B_GROUPS, B_STATE)
    dt = jax.nn.softplus(p_dt.reshape(bsz, length, 2, B_HEADS) + dt_bias)
    return xh, bm, dt[:, :, 0], dt[:, :, 1]


def ssm_c(p_c, conv_w, conv_b, n_seg):
    cm = jax.nn.silu(dwconv_centred(p_c, conv_w[:, XB_W:], conv_b[XB_W:], n_seg))
    bsz, length, _ = cm.shape
    return cm.reshape(bsz, length, B_GROUPS, B_STATE)


def context_states(h_c, w_in, conv_w, conv_b, dt_bias, a_neg):
    xh, bm, dt_f, dt_b = ssm_xb(h_c @ w_in[:, OFF_XB:OFF_C], h_c @ w_in[:, OFF_DT:OFF_GATE],
                                conv_w, conv_b, dt_bias, 1)
    h_f = ssd_final_state(xh, dt_f, a_neg[0], bm)
    h_b = ssd_final_state(flip_seq(xh), flip_seq(dt_b), a_neg[1], flip_seq(bm))
    return h_f, h_b


def token_mixer(h, n_seg, h0_f, h0_b, w_in, b_gate, gmlp_ln_w, gmlp_ln_b, gmlp_ws, gmlp_bs, w_a,
                conv_w, conv_b, a_neg, dt_bias, d_skip, ssm_norm, w_b, w_out):
    proj = h @ w_in
    u = jax.nn.gelu(proj[..., OFF_U:OFF_V])
    v = jax.nn.gelu(proj[..., OFF_V:OFF_Z])
    y_a = chunk_gmlp(u, v, gmlp_ln_w, gmlp_ln_b, gmlp_ws, gmlp_bs) @ w_a
    z = proj[..., OFF_Z:OFF_XB]
    xh, bm, dt_f, dt_b = ssm_xb(proj[..., OFF_XB:OFF_C], proj[..., OFF_DT:OFF_GATE], conv_w, conv_b, dt_bias, n_seg)
    cm = ssm_c(proj[..., OFF_C:OFF_DT], conv_w, conv_b, n_seg)
    y_f, y_b = bidir_ssd(xh, bm, cm, dt_f, dt_b, a_neg, h0_f, h0_b)
    bsz, length = z.shape[:2]
    y = (y_f + y_b + d_skip[:, None] * xh).reshape(bsz, length, B_INNER) * jax.nn.silu(z)
    y = rmsnorm(y.reshape(bsz, length, B_GROUPS, B_INNER // B_GROUPS), ssm_norm.reshape(B_GROUPS, -1))
    y_b_branch = y.reshape(bsz, length, B_INNER) @ w_b
    g = jax.nn.sigmoid(proj[..., OFF_GATE:] + b_gate)
    return (g[..., :D_MODEL] * y_a + g[..., D_MODEL:] * y_b_branch) @ w_out


def setup_inputs(seed: int = 0) -> dict:
    key = jax.random.key(seed)
    ks = iter(jax.random.split(key, 40))
    D = D_MODEL
    L = DEPTH

    def nrm(shape, scale):
        return scale * jax.random.normal(next(ks), shape, jnp.float32)

    def gain(shape):
        return 1.0 + nrm(shape, 0.02)

    dt0 = jnp.exp(jax.random.uniform(next(ks), (L, 2, B_HEADS), jnp.float32,
                                     minval=math.log(1e-3), maxval=math.log(1e-1)))
    return {
        'x': nrm((BATCH, SEQ, D), 1.0),
        'c': nrm((BATCH, D), 1.0),
        'ctx': nrm((BATCH, CTX_LEN, D), 1.0),
        'c_ctx': nrm((D,), 1.0),
        'w_mod': nrm((L, D, 9 * D), 0.5 * D ** -0.5),
        'b_mod': nrm((L, 9 * D), 0.02),
        'norm_ffn1': gain((L, D)),
        'ffn1_gate': nrm((L, D, D_FF), D ** -0.5),
        'ffn1_up': nrm((L, D, D_FF), D ** -0.5),
        'ffn1_down': nrm((L, D_FF, D), D_FF ** -0.5),
        'norm_mix': gain((L, D)),
        'w_in': nrm((L, D, IN_W), D ** -0.5),
        'b_gate': nrm((L, 2 * D), 0.02),
        'gmlp_ln_w': gain((L, A_WIDTH)),
        'gmlp_ln_b': nrm((L, A_WIDTH), 0.02),
        'gmlp_ws': nrm((L, A_GROUPS, A_CHUNK, A_CHUNK), A_CHUNK ** -0.5),
        'gmlp_bs': gain((L, A_GROUPS, A_CHUNK)),
        'w_a': nrm((L, A_WIDTH, D), A_WIDTH ** -0.5),
        'conv_w': nrm((L, B_CONV, XBC_W), B_CONV ** -0.5),
        'conv_b': nrm((L, XBC_W), 0.02),
        'a_log': jnp.log(jax.random.uniform(next(ks), (L, 2, B_HEADS), jnp.float32, minval=1.0, maxval=16.0)),
        'dt_bias': dt0 + jnp.log(-jnp.expm1(-dt0)),
        'd_skip': gain((L, B_HEADS)),
        'ssm_norm': gain((L, B_INNER)),
        'w_b': nrm((L, B_INNER, D), B_INNER ** -0.5),
        'w_out': nrm((L, D, D), D ** -0.5),
        'norm_ffn2': gain((L, D)),
        'ffn2_gate': nrm((L, D, D_FF), D ** -0.5),
        'ffn2_up': nrm((L, D, D_FF), D ** -0.5),
        'ffn2_down': nrm((L, D_FF, D), D_FF ** -0.5),
        'norm_final': gain((D,)),
    }


def reference(x, c, ctx, c_ctx, w_mod, b_mod, norm_ffn1, ffn1_gate, ffn1_up, ffn1_down, norm_mix, w_in, b_gate,
              gmlp_ln_w, gmlp_ln_b, gmlp_ws, gmlp_bs, w_a, conv_w, conv_b, a_log, dt_bias, d_skip, ssm_norm,
              w_b, w_out, norm_ffn2, ffn2_gate, ffn2_up, ffn2_down, norm_final):
    rows = x.shape[1] // GRID_W
    for i in range(DEPTH):
        m = adaln(c, w_mod[i], b_mod[i])
        mc = adaln(c_ctx, w_mod[i], b_mod[i])
        a_neg = -jnp.exp(a_log[i].astype(jnp.float32))
        mix_params = (w_in[i], b_gate[i], gmlp_ln_w[i], gmlp_ln_b[i], gmlp_ws[i], gmlp_bs[i], w_a[i],
                      conv_w[i], conv_b[i], a_neg, dt_bias[i], d_skip[i], ssm_norm[i], w_b[i], w_out[i])
        x = swiglu_half_step(x, m[0:3], norm_ffn1[i], ffn1_gate[i], ffn1_up[i], ffn1_down[i])
        ctx = swiglu_half_step(ctx, mc[0:3], norm_ffn1[i], ffn1_gate[i], ffn1_up[i], ffn1_down[i])
        h_c = modulate(rmsnorm(ctx, norm_mix[i]), mc[3], mc[4])
        h0_f, h0_b = context_states(h_c, w_in[i], conv_w[i], conv_b[i], dt_bias[i], a_neg)
        h = modulate(rmsnorm(x, norm_mix[i]), m[3], m[4])
        x = x + m[5] * token_mixer(h, rows, h0_f, h0_b, *mix_params)
        x = swiglu_half_step(x, m[6:9], norm_ffn2[i], ffn2_gate[i], ffn2_up[i], ffn2_down[i])
        if i < DEPTH - 1:
            zeros = jnp.zeros_like(h0_f)
            ctx = ctx + mc[5] * token_mixer(h_c, 1, zeros, zeros, *mix_params)
            ctx = swiglu_half_step(ctx, mc[6:9], norm_ffn2[i], ffn2_gate[i], ffn2_up[i], ffn2_down[i])
    return rmsnorm(x, norm_final)
```

```python
import functools

import jax
import jax.numpy as jnp
from jax import lax
from jax.experimental import pallas as pl
from jax.experimental.pallas import tpu as pltpu

F32 = jnp.float32
BF16 = jnp.bfloat16
EPS = 1e-6

GRID_W = 64
A_GROUPS = 8
A_CHUNK = 128
HEADDIM = 64
GROUPS = 8
STATE = 128
CONV_K = 5
SSD_CHUNK = 128
N_MOD = 9

LANES = 128
VMEM_LIMIT = 56 * 1024 * 1024


def _cparams(sem):
    return pltpu.CompilerParams(dimension_semantics=sem, vmem_limit_bytes=VMEM_LIMIT)


def _dot(a, b):
    return jnp.dot(a, b, preferred_element_type=F32)


def _split3(a):
    a1 = a.astype(BF16)
    r1 = a - a1.astype(F32)
    a2 = r1.astype(BF16)
    r2 = r1 - a2.astype(F32)
    return a1, a2, r2.astype(BF16)


def _dot_exact_lhs(tri, a):
    a1, a2, a3 = _split3(a)
    return _dot(tri, a1) + _dot(tri, a2) + _dot(tri, a3)


def _dot_exact_rhs(a, tri):
    a1, a2, a3 = _split3(a)
    return _dot(a1, tri) + _dot(a2, tri) + _dot(a3, tri)


def _rms_mod(x, nw, scale, shift):
    ms = jnp.mean(x * x, axis=-1, keepdims=True)
    y = x * lax.rsqrt(ms + EPS) * nw
    return y * (1.0 + scale) + shift


def _adaln_kernel(c_ref, w_ref, b_ref, o_ref):
    s = jax.nn.silu(c_ref[...])
    s1, s2, s3 = _split3(s)
    w = w_ref[...].astype(BF16)
    o_ref[...] = _dot(s1, w) + _dot(s2, w) + _dot(s3, w) + b_ref[...]


def _adaln(c8, w_mod, b_mod):
    d, n = w_mod.shape
    tn = 1024
    return pl.pallas_call(
        _adaln_kernel,
        grid=(n // tn,),
        in_specs=[pl.BlockSpec((8, d), lambda j: (0, 0)),
                  pl.BlockSpec((d, tn), lambda j: (0, j)),
                  pl.BlockSpec((1, tn), lambda j: (0, j))],
        out_specs=pl.BlockSpec((8, tn), lambda j: (0, j)),
        out_shape=jax.ShapeDtypeStruct((8, n), F32),
        compiler_params=_cparams(("arbitrary",)),
        name="adaln",
    )(c8, w_mod, b_mod.reshape(1, n))


def _ffn_kernel(x_ref, sh_ref, sc_ref, gt_ref, nw_ref, wg_ref, wu_ref, wd_ref, fw_ref, o_ref,
                h_scr, acc_scr, *, final_norm):
    j = pl.program_id(1)

    @pl.when(j == 0)
    def _():
        h_scr[...] = _rms_mod(x_ref[...], nw_ref[...], sc_ref[...], sh_ref[...]).astype(BF16)

    h = h_scr[...]
    g = _dot(h, wg_ref[...])
    u = _dot(h, wu_ref[...])
    act = (jax.nn.silu(g) * u).astype(BF16)
    contrib = _dot(act, wd_ref[...])

    @pl.when(j == 0)
    def _():
        acc_scr[...] = contrib

    @pl.when(j > 0)
    def _():
        acc_scr[...] += contrib

    @pl.when(j == pl.num_programs(1) - 1)
    def _():
        out = x_ref[...] + 0.5 * gt_ref[...] * acc_scr[...]
        if final_norm:
            ms = jnp.mean(out * out, axis=-1, keepdims=True)
            out = out * lax.rsqrt(ms + EPS) * fw_ref[...]
        o_ref[...] = out


def _ffn(x2, mod3, mod_k, mod_row, norm_w, wg, wu, wd, final_w, *, tm, tf):
    t, d = x2.shape
    f = wg.shape[1]
    final_norm = final_w is not None
    fw = final_w if final_norm else norm_w

    def mod_spec(k):
        return pl.BlockSpec((None, 1, d), lambda i, j: (mod_row(i), 0, k))

    return pl.pallas_call(
        functools.partial(_ffn_kernel, final_norm=final_norm),
        grid=(t // tm, f // tf),
        in_specs=[pl.BlockSpec((tm, d), lambda i, j: (i, 0)),
                  mod_spec(mod_k), mod_spec(mod_k + 1), mod_spec(mod_k + 2),
                  pl.BlockSpec((1, d), lambda i, j: (0, 0)),
                  pl.BlockSpec((d, tf), lambda i, j: (0, j)),
                  pl.BlockSpec((d, tf), lambda i, j: (0, j)),
                  pl.BlockSpec((tf, d), lambda i, j: (j, 0)),
                  pl.BlockSpec((1, d), lambda i, j: (0, 0))],
        out_specs=pl.BlockSpec((tm, d), lambda i, j: (i, 0)),
        out_shape=jax.ShapeDtypeStruct((t, d), F32),
        scratch_shapes=[pltpu.VMEM((tm, d), BF16), pltpu.VMEM((tm, d), F32)],
        compiler_params=_cparams(("parallel", "arbitrary")),
        name="ffn",
    )(x2, mod3, mod3, mod3, norm_w.reshape(1, d), wg, wu, wd, fw.reshape(1, d))


def _inproj_kernel(x_ref, sh_ref, sc_ref, nw_ref, w_ref, wdt_ref, dtb_ref, cw_ref, cb_ref, bg_ref,
                   p_ref, dt_ref, h_scr, *, seg, j_gelu, j_plain, j_conv):
    j = pl.program_id(1)

    @pl.when(j == 0)
    def _():
        h = _rms_mod(x_ref[...], nw_ref[...], sc_ref[...], sh_ref[...]).astype(BF16)
        h_scr[...] = h
        dt_ref[...] = jax.nn.softplus(_dot(h, wdt_ref[...]) + dtb_ref[...])

    acc = _dot(h_scr[...], w_ref[...])

    @pl.when(j < j_gelu)
    def _():
        p_ref[...] = jax.nn.gelu(acc).astype(BF16)

    @pl.when((j >= j_gelu) & (j < j_plain))
    def _():
        p_ref[...] = acc.astype(BF16)

    @pl.when((j >= j_plain) & (j < j_conv))
    def _():
        tm = acc.shape[0]
        pos = lax.broadcasted_iota(jnp.int32, acc.shape, 0) % seg
        pad = (CONV_K - 1) // 2
        y = cb_ref[...] + cw_ref[pad:pad + 1, :] * acc
        for k in range(CONV_K):
            off = k - pad
            if off == 0:
                continue
            shifted = pltpu.roll(acc, (-off) % tm, axis=0)
            valid = (pos + off >= 0) & (pos + off < seg)
            y = y + cw_ref[k:k + 1, :] * jnp.where(valid, shifted, 0.0)
        p_ref[...] = jax.nn.silu(y).astype(BF16)

    @pl.when(j >= j_conv)
    def _():
        p_ref[...] = jax.nn.sigmoid(acc + bg_ref[...]).astype(BF16)


def _inproj(x2, mod3, mod_row, norm_w, w_main, w_dt, dt_bias, conv_w, conv_b, b_gate, *, seg, tm, tn,
            n_gelu, n_plain, n_conv):
    t, d = x2.shape
    n = w_main.shape[1]
    j_gelu = n_gelu // tn
    j_plain = j_gelu + n_plain // tn
    j_conv = j_plain + n_conv // tn
    nconv_t = n_conv // tn
    ngate_t = b_gate.shape[1] // tn

    def mod_spec(k):
        return pl.BlockSpec((None, 1, d), lambda i, j: (mod_row(i), 0, k))

    return pl.pallas_call(
        functools.partial(_inproj_kernel, seg=seg, j_gelu=j_gelu, j_plain=j_plain, j_conv=j_conv),
        grid=(t // tm, n // tn),
        in_specs=[pl.BlockSpec((tm, d), lambda i, j: (i, 0)),
                  mod_spec(3), mod_spec(4),
                  pl.BlockSpec((1, d), lambda i, j: (0, 0)),
                  pl.BlockSpec((d, tn), lambda i, j: (0, j)),
                  pl.BlockSpec((d, LANES), lambda i, j: (0, 0)),
                  pl.BlockSpec((1, LANES), lambda i, j: (0, 0)),
                  pl.BlockSpec((CONV_K, tn), lambda i, j: (0, jnp.clip(j - j_plain, 0, nconv_t - 1))),
                  pl.BlockSpec((1, tn), lambda i, j: (0, jnp.clip(j - j_plain, 0, nconv_t - 1))),
                  pl.BlockSpec((1, tn), lambda i, j: (0, jnp.clip(j - j_conv, 0, ngate_t - 1)))],
        out_specs=[pl.BlockSpec((tm, tn), lambda i, j: (i, j)),
                   pl.BlockSpec((tm, LANES), lambda i, j: (i, 0))],
        out_shape=[jax.ShapeDtypeStruct((t, n), BF16), jax.ShapeDtypeStruct((t, LANES), F32)],
        scratch_shapes=[pltpu.VMEM((tm, d), BF16)],
        compiler_params=_cparams(("parallel", "arbitrary")),
        name="inproj",
    )(x2, mod3, mod3, norm_w.reshape(1, d), w_main, w_dt, dt_bias, conv_w, conv_b, b_gate)


def _gmlp_kernel(u_ref, v_ref, lw_ref, lb_ref, ws_ref, bs_ref, o_ref):
    v = v_ref[...].astype(F32)
    mu = jnp.mean(v, axis=-1, keepdims=True)
    vc = v - mu
    var = jnp.mean(vc * vc, axis=-1, keepdims=True)
    vn = (vc * lax.rsqrt(var + EPS) * lw_ref[...] + lb_ref[...]).astype(BF16)
    tm, width = vn.shape
    gw = width // A_GROUPS
    for c in range(tm // A_CHUNK):
        r0 = c * A_CHUNK
        for g in range(A_GROUPS):
            c0 = g * gw
            s = _dot(ws_ref[g], vn[r0:r0 + A_CHUNK, c0:c0 + gw]) + bs_ref[:, g:g + 1]
            u = u_ref[r0:r0 + A_CHUNK, c0:c0 + gw].astype(F32)
            o_ref[r0:r0 + A_CHUNK, c0:c0 + gw] = (u * s).astype(BF16)


def _gmlp(p, width, ln_w, ln_b, ws, bs_t, *, tm):
    t = p.shape[0]
    return pl.pallas_call(
        _gmlp_kernel,
        grid=(t // tm,),
        in_specs=[pl.BlockSpec((tm, width), lambda i: (i, 0)),
                  pl.BlockSpec((tm, width), lambda i: (i, 1)),
                  pl.BlockSpec((1, width), lambda i: (0, 0)),
                  pl.BlockSpec((1, width), lambda i: (0, 0)),
                  pl.BlockSpec((A_GROUPS, A_CHUNK, A_CHUNK), lambda i: (0, 0, 0)),
                  pl.BlockSpec((A_CHUNK, A_GROUPS), lambda i: (0, 0))],
        out_specs=pl.BlockSpec((tm, width), lambda i: (i, 0)),
        out_shape=jax.ShapeDtypeStruct((t, width), BF16),
        compiler_params=_cparams(("parallel",)),
        name="gmlp",
    )(p, p, ln_w.reshape(1, width), ln_b.reshape(1, width), ws, bs_t)


HPG = 8
GL = 2 * HPG


def _decay_tables(dt, alog_row, cs_scr, cst_scr, dtt_scr):
    q = dt.shape[0]
    a = dt * (-jnp.exp(alog_row))
    ri = lax.broadcasted_iota(jnp.int32, (q, q), 0)
    ci = lax.broadcasted_iota(jnp.int32, (q, q), 1)
    tril = jnp.where(ri >= ci, 1.0, 0.0).astype(BF16)
    triu = jnp.where(ri <= ci, 1.0, 0.0).astype(BF16)
    lane_f = (lax.broadcasted_iota(jnp.int32, (q, LANES), 1) % GL) < HPG
    sub_f = (lax.broadcasted_iota(jnp.int32, (LANES, q), 0) % GL) < HPG
    cs = jnp.where(lane_f, _dot_exact_lhs(tril, a), _dot_exact_lhs(triu, a))
    a_t = a.T
    cst = jnp.where(sub_f, _dot_exact_rhs(a_t, triu), _dot_exact_rhs(a_t, tril))
    dtt = dt.T
    for g in range(GROUPS):
        cs_scr[g] = cs[:, g * GL:(g + 1) * GL]
        cst_scr[g] = cst[g * GL:(g + 1) * GL, :]
        dtt_scr[g] = dtt[g * GL:(g + 1) * GL, :]


def _block_diag_pair(xp):
    lane = lax.broadcasted_iota(jnp.int32, xp.shape, 1)
    xf = xp.astype(F32)
    lo = jnp.where(lane < HEADDIM, xf, 0.0).astype(BF16)
    hi = jnp.where(lane >= HEADDIM, xf, 0.0).astype(BF16)
    return jnp.concatenate([lo, hi], axis=0)


def _head_terms(cs_g, cst_g, dtt_g, hl, last):
    q = cs_g.shape[0]
    colb = jnp.broadcast_to(cs_g[:, hl:hl + 1], (q, q))
    rowb = cst_g[hl:hl + 1, :]
    dtrow = dtt_g[hl:hl + 1, :]
    clast = colb[last:last + 1, :]
    return colb, rowb, dtrow, clast


def _ctx_state_kernel(x_ref, b_ref, dt_ref, alog_ref, h_ref, cs_scr, cst_scr, dtt_scr):
    q = x_ref.shape[0]
    _decay_tables(dt_ref[...], alog_ref[...], cs_scr, cst_scr, dtt_scr)
    lane_row = lax.broadcasted_iota(jnp.int32, (1, LANES), 1)

    def group_body(g, carry):
        cs_g, cst_g, dtt_g = cs_scr[g], cst_scr[g], dtt_scr[g]
        bg_t = b_ref[:, pl.ds(pl.multiple_of(g * STATE, STATE), STATE)].astype(F32).T
        for d in range(2):
            last = q - 1 if d == 0 else 0
            for pair in range(HPG // 2):
                col0 = pl.multiple_of(g * (HPG * HEADDIM) + pair * LANES, LANES)
                bd = _block_diag_pair(x_ref[:, pl.ds(col0, LANES)])
                bts = []
                for k in range(2):
                    _, rowb, dtrow, clast = _head_terms(cs_g, cst_g, dtt_g, d * HPG + pair * 2 + k, last)
                    bts.append((bg_t * (jnp.exp(clast - rowb) * dtrow)).astype(BF16))
                h_ref[d, g, :, pair * LANES:(pair + 1) * LANES] = _dot(jnp.concatenate(bts, axis=1), bd)
        return carry

    lax.fori_loop(0, GROUPS, group_body, 0)
    del lane_row


def _ctx_states(pc, dtc, alog_row, *, bsz, q, x_blk, b_blk):
    inner = GROUPS * HPG * HEADDIM
    return pl.pallas_call(
        _ctx_state_kernel,
        grid=(bsz,),
        in_specs=[pl.BlockSpec((q, inner), lambda b: (b, x_blk)),
                  pl.BlockSpec((q, GROUPS * STATE), lambda b: (b, b_blk)),
                  pl.BlockSpec((q, LANES), lambda b: (b, 0)),
                  pl.BlockSpec((1, LANES), lambda b: (0, 0))],
        out_specs=pl.BlockSpec((2, None, GROUPS, STATE, HPG * HEADDIM), lambda b: (0, b, 0, 0, 0)),
        out_shape=jax.ShapeDtypeStruct((2, bsz, GROUPS, STATE, HPG * HEADDIM), F32),
        scratch_shapes=[pltpu.VMEM((GROUPS, q, GL), F32), pltpu.VMEM((GROUPS, GL, q), F32),
                        pltpu.VMEM((GROUPS, GL, q), F32)],
        compiler_params=_cparams(("arbitrary",)),
        name="ctx_state",
    )(pc, pc, dtc, alog_row)


def _ssd_kernel(xf_ref, bf_ref, cf_ref, dtf_ref, xb_ref, bb_ref, cb_ref, dtb_ref, alog_ref, dskip_ref, h0_ref,
                yf_ref, yb_ref, s_scr, cs_scr, cst_scr, dtt_scr):
    q = xf_ref.shape[0]

    @pl.when(pl.program_id(1) == 0)
    def _():
        s_scr[...] = h0_ref[...]

    lane_f = (lax.broadcasted_iota(jnp.int32, (q, LANES), 1) % GL) < HPG
    dt = jnp.where(lane_f, dtf_ref[...], dtb_ref[...])
    _decay_tables(dt, alog_ref[...], cs_scr, cst_scr, dtt_scr)

    ri = lax.broadcasted_iota(jnp.int32, (q, q), 0)
    ci = lax.broadcasted_iota(jnp.int32, (q, q), 1)
    lane_lo = lax.broadcasted_iota(jnp.int32, (q, LANES), 1) < HEADDIM
    lane_lo_row = lax.broadcasted_iota(jnp.int32, (1, LANES), 1) < HEADDIM
    dirs = ((xf_ref, bf_ref, cf_ref, yf_ref), (xb_ref, bb_ref, cb_ref, yb_ref))

    def group_body(g, carry):
        cs_g, cst_g, dtt_g = cs_scr[g], cst_scr[g], dtt_scr[g]
        for d, (x_ref, b_ref, c_ref, y_ref) in enumerate(dirs):
            last = q - 1 if d == 0 else 0
            mask = (ri >= ci) if d == 0 else (ri <= ci)
            gcol = pl.ds(pl.multiple_of(g * STATE, STATE), STATE)
            bg = b_ref[:, gcol]
            cg = c_ref[:, gcol]
            cb = lax.dot_general(cg, bg, (((1,), (1,)), ((), ())), preferred_element_type=F32)
            bg_t = bg.astype(F32).T
            s_g = s_scr[d, g]
            y_off = _dot(cg, s_g.astype(BF16))
            for pair in range(HPG // 2):
                col0 = pl.multiple_of(g * (HPG * HEADDIM) + pair * LANES, LANES)
                xp = x_ref[:, pl.ds(col0, LANES)]
                bd = _block_diag_pair(xp)
                ms, bts, ecols, elasts = [], [], [], []
                for k in range(2):
                    colb, rowb, dtrow, clast = _head_terms(cs_g, cst_g, dtt_g, d * HPG + pair * 2 + k, last)
                    decay = jnp.exp(jnp.where(mask, colb - rowb, -jnp.inf))
                    ms.append((cb * decay * dtrow).astype(BF16))
                    bts.append((bg_t * (jnp.exp(clast - rowb) * dtrow)).astype(BF16))
                    ecols.append(jnp.exp(colb))
                    elasts.append(jnp.exp(clast))
                psl = slice(pair * LANES, (pair + 1) * LANES)
                y = _dot(jnp.concatenate(ms, axis=1), bd)
                y = y + y_off[:, psl] * jnp.where(lane_lo, ecols[0], ecols[1])
                if d == 0:
                    y = y + xp.astype(F32) * dskip_ref[:, pl.ds(col0, LANES)]
                y_ref[:, pl.ds(col0, LANES)] = y.astype(BF16)
                grow = jnp.where(lane_lo_row, elasts[0], elasts[1])
                s_scr[d, g, :, psl] = s_g[:, psl] * grow + _dot(jnp.concatenate(bts, axis=1), bd)
        return carry

    lax.fori_loop(0, GROUPS, group_body, 0)


def _ssd(p, dt, alog_row, dskip_row, h0, *, bsz, nc, x_blk, b_blk, c_blk):
    q = SSD_CHUNK
    t = p.shape[0]
    inner = GROUPS * HPG * HEADDIM
    gs = GROUPS * STATE

    def fwd(b, c):
        return b * nc + c

    def bwd(b, c):
        return b * nc + (nc - 1 - c)

    def specs(row):
        return [pl.BlockSpec((q, inner), lambda b, c: (row(b, c), x_blk)),
                pl.BlockSpec((q, gs), lambda b, c: (row(b, c), b_blk)),
                pl.BlockSpec((q, gs), lambda b, c: (row(b, c), c_blk)),
                pl.BlockSpec((q, LANES), lambda b, c: (row(b, c), 0))]

    return pl.pallas_call(
        _ssd_kernel,
        grid=(bsz, nc),
        in_specs=specs(fwd) + specs(bwd) + [
            pl.BlockSpec((1, LANES), lambda b, c: (0, 0)),
            pl.BlockSpec((1, inner), lambda b, c: (0, 0)),
            pl.BlockSpec((2, None, GROUPS, STATE, HPG * HEADDIM), lambda b, c: (0, b, 0, 0, 0))],
        out_specs=[pl.BlockSpec((q, inner), lambda b, c: (fwd(b, c), 0)),
                   pl.BlockSpec((q, inner), lambda b, c: (bwd(b, c), 0))],
        out_shape=[jax.ShapeDtypeStruct((t, inner), BF16), jax.ShapeDtypeStruct((t, inner), BF16)],
        scratch_shapes=[pltpu.VMEM((2, GROUPS, STATE, HPG * HEADDIM), F32),
                        pltpu.VMEM((GROUPS, q, GL), F32), pltpu.VMEM((GROUPS, GL, q), F32),
                        pltpu.VMEM((GROUPS, GL, q), F32)],
        compiler_params=_cparams(("arbitrary", "arbitrary")),
        name="ssd",
    )(p, p, p, dt, p, p, p, dt, alog_row, dskip_row, h0)


def _gnorm_kernel(yf_ref, yb_ref, z_ref, w_ref, o_ref):
    y = (yf_ref[...].astype(F32) + yb_ref[...].astype(F32)) * jax.nn.silu(z_ref[...].astype(F32))
    gw = y.shape[1] // GROUPS
    for g in range(GROUPS):
        yg = y[:, g * gw:(g + 1) * gw]
        ms = jnp.mean(yg * yg, axis=-1, keepdims=True)
        o_ref[:, g * gw:(g + 1) * gw] = (yg * lax.rsqrt(ms + EPS) * w_ref[:, g * gw:(g + 1) * gw]).astype(BF16)


def _gnorm(yf, yb, p, z_blk, norm_w, *, tm):
    t, inner = yf.shape
    return pl.pallas_call(
        _gnorm_kernel,
        grid=(t // tm,),
        in_specs=[pl.BlockSpec((tm, inner), lambda i: (i, 0)),
                  pl.BlockSpec((tm, inner), lambda i: (i, 0)),
                  pl.BlockSpec((tm, inner), lambda i: (i, z_blk)),
                  pl.BlockSpec((1, inner), lambda i: (0, 0))],
        out_specs=pl.BlockSpec((tm, inner), lambda i: (i, 0)),
        out_shape=jax.ShapeDtypeStruct((t, inner), BF16),
        compiler_params=_cparams(("parallel",)),
        name="gnorm",
    )(yf, yb, p, norm_w.reshape(1, inner))


def _mixout_kernel(us_ref, yn_ref, ga_ref, gb_ref, x_ref, m_ref, wa_ref, wb_ref, wo_ref, o_ref, acc_scr):
    j = pl.program_id(1)
    ya = _dot(us_ref[...], wa_ref[...])
    yb = _dot(yn_ref[...], wb_ref[...])
    merged = (ga_ref[...].astype(F32) * ya + gb_ref[...].astype(F32) * yb).astype(BF16)
    contrib = _dot(merged, wo_ref[...])

    @pl.when(j == 0)
    def _():
        acc_scr[...] = contrib

    @pl.when(j > 0)
    def _():
        acc_scr[...] += contrib

    @pl.when(j == pl.num_programs(1) - 1)
    def _():
        o_ref[...] = x_ref[...] + m_ref[...] * acc_scr[...]


def _mixout(us, yn, p, gate_col0, x2, mod3, mod_row, wa, wb, wo, *, tm, tn):
    t, d = x2.shape
    wa_in = wa.shape[0]
    wb_in = wb.shape[0]
    ga_blk = gate_col0 // tn
    gb_blk = (gate_col0 + d) // tn
    return pl.pallas_call(
        _mixout_kernel,
        grid=(t // tm, d // tn),
        in_specs=[pl.BlockSpec((tm, wa_in), lambda i, j: (i, 0)),
                  pl.BlockSpec((tm, wb_in), lambda i, j: (i, 0)),
                  pl.BlockSpec((tm, tn), lambda i, j: (i, ga_blk + j)),
                  pl.BlockSpec((tm, tn), lambda i, j: (i, gb_blk + j)),
                  pl.BlockSpec((tm, d), lambda i, j: (i, 0)),
                  pl.BlockSpec((None, 1, d), lambda i, j: (mod_row(i), 0, 5)),
                  pl.BlockSpec((wa_in, tn), lambda i, j: (0, j)),
                  pl.BlockSpec((wb_in, tn), lambda i, j: (0, j)),
                  pl.BlockSpec((tn, d), lambda i, j: (j, 0))],
        out_specs=pl.BlockSpec((tm, d), lambda i, j: (i, 0)),
        out_shape=jax.ShapeDtypeStruct((t, d), F32),
        scratch_shapes=[pltpu.VMEM((tm, d), F32)],
        compiler_params=_cparams(("parallel", "arbitrary")),
        name="mixout",
    )(us, yn, p, p, x2, mod3, wa, wb, wo)


def _dt_perm(a):
    lead = a.shape[:-2]
    a = a.reshape(*lead, 2, GROUPS, HPG)
    a = jnp.swapaxes(a, -3, -2)
    return a.reshape(*lead, 2 * GROUPS * HPG)


def kernel(x, c, ctx, c_ctx, w_mod, b_mod, norm_ffn1, ffn1_gate, ffn1_up, ffn1_down, norm_mix, w_in, b_gate,
           gmlp_ln_w, gmlp_ln_b, gmlp_ws, gmlp_bs, w_a, conv_w, conv_b, a_log, dt_bias, d_skip, ssm_norm,
           w_b, w_out, norm_ffn2, ffn2_gate, ffn2_up, ffn2_down, norm_final):
    bsz, seq, d = x.shape
    ctx_len = ctx.shape[1]
    depth = w_mod.shape[0]
    assert depth == 1, "context stream update between layers is not implemented"
    assert bsz < 8
    inner = GROUPS * HPG * HEADDIM
    gs = GROUPS * STATE
    a_width = w_a.shape[1]
    n_heads = GROUPS * HPG
    assert w_b.shape[1] == inner and conv_w.shape[2] == inner + 2 * gs
    assert w_in.shape[2] == 2 * a_width + 2 * inner + 2 * gs + 2 * n_heads + 2 * d

    t = bsz * seq
    tc = bsz * ctx_len
    x2 = x.reshape(t, d)
    ctx2 = ctx.reshape(tc, d)

    off_dt = 2 * a_width + 2 * inner + 2 * gs
    off_gate = off_dt + 2 * n_heads
    n_gelu, n_plain, n_conv = 2 * a_width, inner, inner + 2 * gs
    col_z = n_gelu
    col_x = n_gelu + n_plain
    col_b = col_x + inner
    col_c = col_b + gs
    col_gate = n_gelu + n_plain + n_conv

    i = 0
    w_main = jnp.concatenate([w_in[i][:, :off_dt], w_in[i][:, off_gate:]], axis=1).astype(BF16)
    w_dt = _dt_perm(w_in[i][:, off_dt:off_gate].reshape(d, 2, n_heads)).astype(BF16)
    dt_bias_row = _dt_perm(dt_bias[i]).reshape(1, 2 * n_heads)
    alog_row = _dt_perm(a_log[i].astype(F32)).reshape(1, 2 * n_heads)
    dskip_row = jnp.repeat(d_skip[i], HEADDIM).reshape(1, inner)
    bf = lambda w: w.astype(BF16)

    c8 = jnp.zeros((8, d), F32).at[:bsz].set(c).at[bsz].set(c_ctx)
    mod3 = _adaln(c8, w_mod[i], b_mod[i]).reshape(8, 1, N_MOD * d)

    tm = 512
    lat_row = lambda ti: (ti * tm) // seq
    ctx_row = lambda ti: bsz

    ffn1_w = (bf(ffn1_gate[i]), bf(ffn1_up[i]), bf(ffn1_down[i]))
    x1 = _ffn(x2, mod3, 0, lat_row, norm_ffn1[i], *ffn1_w, None, tm=tm, tf=512)
    ctx1 = _ffn(ctx2, mod3, 0, ctx_row, norm_ffn1[i], *ffn1_w, None, tm=tm, tf=512)

    proj_args = (norm_mix[i], w_main, w_dt, dt_bias_row, conv_w[i], conv_b[i].reshape(1, -1), b_gate[i].reshape(1, -1))
    proj_kw = dict(tm=tm, tn=1024, n_gelu=n_gelu, n_plain=n_plain, n_conv=n_conv)
    pc, dtc = _inproj(ctx1, mod3, ctx_row, *proj_args, seg=ctx_len, **proj_kw)
    h0 = _ctx_states(pc, dtc, alog_row, bsz=bsz, q=ctx_len, x_blk=col_x // inner, b_blk=col_b // gs)

    p, dt = _inproj(x1, mod3, lat_row, *proj_args, seg=GRID_W, **proj_kw)
    us = _gmlp(p, a_width, gmlp_ln_w[i], gmlp_ln_b[i], bf(gmlp_ws[i]), gmlp_bs[i].T, tm=tm)
    yf, yb = _ssd(p, dt, alog_row, dskip_row, h0, bsz=bsz, nc=seq // SSD_CHUNK,
                  x_blk=col_x // inner, b_blk=col_b // gs, c_blk=col_c // gs)
    yn = _gnorm(yf, yb, p, col_z // inner, ssm_norm[i], tm=256)
    x3 = _mixout(us, yn, p, col_gate, x1, mod3, lat_row, bf(w_a[i]), bf(w_b[i]), bf(w_out[i]), tm=tm, tn=256)

    out = _ffn(x3, mod3, 6, lat_row, norm_ffn2[i], bf(ffn2_gate[i]), bf(ffn2_up[i]), bf(ffn2_down[i]),
               norm_final, tm=tm, tf=512)
    return out.reshape(bsz, seq, d)
```

```python
import functools

import jax
import jax.numpy as jnp
from jax import lax
from jax.experimental import pallas as pl
from jax.experimental.pallas import tpu as pltpu

F32 = jnp.float32
BF16 = jnp.bfloat16
EPS = 1e-6

GRID_W = 64
A_GROUPS = 8
A_CHUNK = 128
HEADDIM = 64
GROUPS = 8
STATE = 128
CONV_K = 5
SSD_CHUNK = 128
N_MOD = 9

LANES = 128
SUB = 256
OUT_SUB = 512
VMEM_LIMIT = 56 * 1024 * 1024


def _cparams(sem):
    return pltpu.CompilerParams(dimension_semantics=sem, vmem_limit_bytes=VMEM_LIMIT)


def _dot(a, b):
    return jnp.dot(a, b, preferred_element_type=F32)


def _split3(a):
    a1 = a.astype(BF16)
    r1 = a - a1.astype(F32)
    a2 = r1.astype(BF16)
    r2 = r1 - a2.astype(F32)
    return a1, a2, r2.astype(BF16)


def _dot_exact_lhs(tri, a):
    a1, a2, a3 = _split3(a)
    return _dot(tri, a1) + _dot(tri, a2) + _dot(tri, a3)


def _dot_exact_rhs(a, tri):
    a1, a2, a3 = _split3(a)
    return _dot(a1, tri) + _dot(a2, tri) + _dot(a3, tri)


def _rms_mod(x, nw, scale, shift):
    ms = jnp.mean(x * x, axis=-1, keepdims=True)
    y = x * lax.rsqrt(ms + EPS) * nw
    return y * (1.0 + scale) + shift


def _adaln_kernel(c_ref, w_ref, b_ref, o_ref):
    s = jax.nn.silu(c_ref[...])
    s1, s2, s3 = _split3(s)
    w = w_ref[...].astype(BF16)
    o_ref[...] = _dot(s1, w) + _dot(s2, w) + _dot(s3, w) + b_ref[...]


def _adaln(c8, w_mod, b_mod):
    d, n = w_mod.shape
    tn = 1024
    return pl.pallas_call(
        _adaln_kernel,
        grid=(n // tn,),
        in_specs=[pl.BlockSpec((8, d), lambda j: (0, 0)),
                  pl.BlockSpec((d, tn), lambda j: (0, j)),
                  pl.BlockSpec((1, tn), lambda j: (0, j))],
        out_specs=pl.BlockSpec((8, tn), lambda j: (0, j)),
        out_shape=jax.ShapeDtypeStruct((8, n), F32),
        compiler_params=_cparams(("arbitrary",)),
        name="adaln",
    )(c8, w_mod, b_mod.reshape(1, n))


def _ffn_kernel(x_ref, sh_ref, sc_ref, gt_ref, nw_ref, wg_ref, wu_ref, wd_ref, fw_ref, o_ref,
                h_scr, acc_scr, *, final_norm):
    j = pl.program_id(1)

    @pl.when(j == 0)
    def _():
        h_scr[...] = _rms_mod(x_ref[...], nw_ref[...], sc_ref[...], sh_ref[...]).astype(BF16)
        acc_scr[...] = jnp.zeros_like(acc_scr)

    acts = []
    for s in range(wg_ref.shape[1] // SUB):
        cols = slice(s * SUB, (s + 1) * SUB)
        g = _dot(h_scr[...], wg_ref[:, cols])
        u = _dot(h_scr[...], wu_ref[:, cols])
        acts.append((jax.nn.silu(g) * u).astype(BF16))
    acc_scr[...] += _dot(jnp.concatenate(acts, axis=1), wd_ref[...])

    @pl.when(j == pl.num_programs(1) - 1)
    def _():
        out = x_ref[...] + 0.5 * gt_ref[...] * acc_scr[...]
        if final_norm:
            ms = jnp.mean(out * out, axis=-1, keepdims=True)
            out = out * lax.rsqrt(ms + EPS) * fw_ref[...]
        o_ref[...] = out


def _ffn(x2, mod3, mod_k, mod_row, norm_w, wg, wu, wd, final_w, *, tm, tf):
    t, d = x2.shape
    f = wg.shape[1]
    final_norm = final_w is not None
    fw = final_w if final_norm else norm_w

    def mod_spec(k):
        return pl.BlockSpec((None, 1, d), lambda i, j: (mod_row(i), 0, k))

    return pl.pallas_call(
        functools.partial(_ffn_kernel, final_norm=final_norm),
        grid=(t // tm, f // tf),
        in_specs=[pl.BlockSpec((tm, d), lambda i, j: (i, 0)),
                  mod_spec(mod_k), mod_spec(mod_k + 1), mod_spec(mod_k + 2),
                  pl.BlockSpec((1, d), lambda i, j: (0, 0)),
                  pl.BlockSpec((d, tf), lambda i, j: (0, j)),
                  pl.BlockSpec((d, tf), lambda i, j: (0, j)),
                  pl.BlockSpec((tf, d), lambda i, j: (j, 0)),
                  pl.BlockSpec((1, d), lambda i, j: (0, 0))],
        out_specs=pl.BlockSpec((tm, d), lambda i, j: (i, 0)),
        out_shape=jax.ShapeDtypeStruct((t, d), F32),
        scratch_shapes=[pltpu.VMEM((tm, d), BF16), pltpu.VMEM((tm, d), F32)],
        compiler_params=_cparams(("parallel", "arbitrary")),
        name="ffn",
    )(x2, mod3, mod3, mod3, norm_w.reshape(1, d), wg, wu, wd, fw.reshape(1, d))


def _inproj_kernel(x_ref, sh_ref, sc_ref, nw_ref, w_ref, wdt_ref, dtb_ref, cw_ref, cb_ref, bg_ref,
                   p_ref, dt_ref, h_scr, *, seg, j_gelu, j_plain, j_conv):
    j = pl.program_id(1)
    tm, tn = p_ref.shape

    @pl.when(j == 0)
    def _():
        h = _rms_mod(x_ref[...], nw_ref[...], sc_ref[...], sh_ref[...]).astype(BF16)
        h_scr[...] = h
        dt_ref[...] = jax.nn.softplus(_dot(h, wdt_ref[...]) + dtb_ref[...])

    def run(epilogue):
        for s in range(tn // SUB):
            cols = slice(s * SUB, (s + 1) * SUB)
            p_ref[:, cols] = epilogue(_dot(h_scr[...], w_ref[:, cols]), cols).astype(BF16)

    @pl.when(j < j_gelu)
    def _():
        run(lambda acc, cols: jax.nn.gelu(acc))

    @pl.when((j >= j_gelu) & (j < j_plain))
    def _():
        run(lambda acc, cols: acc)

    @pl.when((j >= j_plain) & (j < j_conv))
    def _():
        pad = (CONV_K - 1) // 2
        pos = lax.broadcasted_iota(jnp.int32, (tm, SUB), 0) % seg
        offs = [k - pad for k in range(CONV_K) if k != pad]
        valid = {off: (pos + off >= 0) & (pos + off < seg) for off in offs}

        def conv_silu(acc, cols):
            y = cb_ref[:, cols] + cw_ref[pad:pad + 1, cols] * acc
            for off in offs:
                shifted = pltpu.roll(acc, (-off) % tm, axis=0)
                y = y + cw_ref[pad + off:pad + off + 1, cols] * jnp.where(valid[off], shifted, 0.0)
            return jax.nn.silu(y)

        run(conv_silu)

    @pl.when(j >= j_conv)
    def _():
        run(lambda acc, cols: jax.nn.sigmoid(acc + bg_ref[:, cols]))


def _inproj(x2, mod3, mod_row, norm_w, w_main, w_dt, dt_bias, conv_w, conv_b, b_gate, *, seg, tm, tn,
            n_gelu, n_plain, n_conv):
    t, d = x2.shape
    n = w_main.shape[1]
    j_gelu = n_gelu // tn
    j_plain = j_gelu + n_plain // tn
    j_conv = j_plain + n_conv // tn
    nconv_t = n_conv // tn
    ngate_t = b_gate.shape[1] // tn

    def mod_spec(k):
        return pl.BlockSpec((None, 1, d), lambda i, j: (mod_row(i), 0, k))

    return pl.pallas_call(
        functools.partial(_inproj_kernel, seg=seg, j_gelu=j_gelu, j_plain=j_plain, j_conv=j_conv),
        grid=(t // tm, n // tn),
        in_specs=[pl.BlockSpec((tm, d), lambda i, j: (i, 0)),
                  mod_spec(3), mod_spec(4),
                  pl.BlockSpec((1, d), lambda i, j: (0, 0)),
                  pl.BlockSpec((d, tn), lambda i, j: (0, j)),
                  pl.BlockSpec((d, LANES), lambda i, j: (0, 0)),
                  pl.BlockSpec((1, LANES), lambda i, j: (0, 0)),
                  pl.BlockSpec((CONV_K, tn), lambda i, j: (0, jnp.clip(j - j_plain, 0, nconv_t - 1))),
                  pl.BlockSpec((1, tn), lambda i, j: (0, jnp.clip(j - j_plain, 0, nconv_t - 1))),
                  pl.BlockSpec((1, tn), lambda i, j: (0, jnp.clip(j - j_conv, 0, ngate_t - 1)))],
        out_specs=[pl.BlockSpec((tm, tn), lambda i, j: (i, j)),
                   pl.BlockSpec((tm, LANES), lambda i, j: (i, 0))],
        out_shape=[jax.ShapeDtypeStruct((t, n), BF16), jax.ShapeDtypeStruct((t, LANES), F32)],
        scratch_shapes=[pltpu.VMEM((tm, d), BF16)],
        compiler_params=_cparams(("parallel", "arbitrary")),
        name="inproj",
    )(x2, mod3, mod3, norm_w.reshape(1, d), w_main, w_dt, dt_bias, conv_w, conv_b, b_gate)


def _gmlp_kernel(u_ref, v_ref, lw_ref, lb_ref, ws_ref, bs_ref, o_ref):
    v = v_ref[...].astype(F32)
    mu = jnp.mean(v, axis=-1, keepdims=True)
    vc = v - mu
    var = jnp.mean(vc * vc, axis=-1, keepdims=True)
    vn = (vc * lax.rsqrt(var + EPS) * lw_ref[...] + lb_ref[...]).astype(BF16)
    tm, width = vn.shape
    gw = width // A_GROUPS
    for c in range(tm // A_CHUNK):
        r0 = c * A_CHUNK
        for g in range(A_GROUPS):
            c0 = g * gw
            s = _dot(ws_ref[g], vn[r0:r0 + A_CHUNK, c0:c0 + gw]) + bs_ref[:, g:g + 1]
            u = u_ref[r0:r0 + A_CHUNK, c0:c0 + gw].astype(F32)
            o_ref[r0:r0 + A_CHUNK, c0:c0 + gw] = (u * s).astype(BF16)


def _gmlp(p, width, ln_w, ln_b, ws, bs_t, *, tm):
    t = p.shape[0]
    return pl.pallas_call(
        _gmlp_kernel,
        grid=(t // tm,),
        in_specs=[pl.BlockSpec((tm, width), lambda i: (i, 0)),
                  pl.BlockSpec((tm, width), lambda i: (i, 1)),
                  pl.BlockSpec((1, width), lambda i: (0, 0)),
                  pl.BlockSpec((1, width), lambda i: (0, 0)),
                  pl.BlockSpec((A_GROUPS, A_CHUNK, A_CHUNK), lambda i: (0, 0, 0)),
                  pl.BlockSpec((A_CHUNK, A_GROUPS), lambda i: (0, 0))],
        out_specs=pl.BlockSpec((tm, width), lambda i: (i, 0)),
        out_shape=jax.ShapeDtypeStruct((t, width), BF16),
        compiler_params=_cparams(("parallel",)),
        name="gmlp",
    )(p, p, ln_w.reshape(1, width), ln_b.reshape(1, width), ws, bs_t)


HPG = 8
GL = 2 * HPG


def _decay_tables(dt, alog_row, cs_scr, cst_scr, dtt_scr):
    q = dt.shape[0]
    a = dt * (-jnp.exp(alog_row))
    ri = lax.broadcasted_iota(jnp.int32, (q, q), 0)
    ci = lax.broadcasted_iota(jnp.int32, (q, q), 1)
    tril = jnp.where(ri >= ci, 1.0, 0.0).astype(BF16)
    triu = jnp.where(ri <= ci, 1.0, 0.0).astype(BF16)
    lane_f = (lax.broadcasted_iota(jnp.int32, (q, LANES), 1) % GL) < HPG
    sub_f = (lax.broadcasted_iota(jnp.int32, (LANES, q), 0) % GL) < HPG
    cs = jnp.where(lane_f, _dot_exact_lhs(tril, a), _dot_exact_lhs(triu, a))
    a_t = a.T
    cst = jnp.where(sub_f, _dot_exact_rhs(a_t, triu), _dot_exact_rhs(a_t, tril))
    dtt = dt.T
    for g in range(GROUPS):
        cs_scr[g] = cs[:, g * GL:(g + 1) * GL]
        cst_scr[g] = cst[g * GL:(g + 1) * GL, :]
        dtt_scr[g] = dtt[g * GL:(g + 1) * GL, :]


def _block_diag_pair(xp):
    lane = lax.broadcasted_iota(jnp.int32, xp.shape, 1)
    xf = xp.astype(F32)
    lo = jnp.where(lane < HEADDIM, xf, 0.0).astype(BF16)
    hi = jnp.where(lane >= HEADDIM, xf, 0.0).astype(BF16)
    return jnp.concatenate([lo, hi], axis=0)


def _head_terms(cs_g, cst_g, dtt_g, hl, last):
    q = cs_g.shape[0]
    colb = jnp.broadcast_to(cs_g[:, hl:hl + 1], (q, q))
    rowb = cst_g[hl:hl + 1, :]
    dtrow = dtt_g[hl:hl + 1, :]
    clast = colb[last:last + 1, :]
    return colb, rowb, dtrow, clast


def _ctx_state_kernel(x_ref, b_ref, dt_ref, alog_ref, h_ref, cs_scr, cst_scr, dtt_scr):
    q = x_ref.shape[0]
    _decay_tables(dt_ref[...], alog_ref[...], cs_scr, cst_scr, dtt_scr)
    lane_row = lax.broadcasted_iota(jnp.int32, (1, LANES), 1)

    def group_body(g, carry):
        cs_g, cst_g, dtt_g = cs_scr[g], cst_scr[g], dtt_scr[g]
        bg_t = b_ref[:, pl.ds(pl.multiple_of(g * STATE, STATE), STATE)].astype(F32).T
        for d in range(2):
            last = q - 1 if d == 0 else 0
            for pair in range(HPG // 2):
                col0 = pl.multiple_of(g * (HPG * HEADDIM) + pair * LANES, LANES)
                bd = _block_diag_pair(x_ref[:, pl.ds(col0, LANES)])
                bts = []
                for k in range(2):
                    _, rowb, dtrow, clast = _head_terms(cs_g, cst_g, dtt_g, d * HPG + pair * 2 + k, last)
                    bts.append((bg_t * (jnp.exp(clast - rowb) * dtrow)).astype(BF16))
                h_ref[d, g, :, pair * LANES:(pair + 1) * LANES] = _dot(jnp.concatenate(bts, axis=1), bd)
        return carry

    lax.fori_loop(0, GROUPS, group_body, 0)
    del lane_row


def _ctx_states(pc, dtc, alog_row, *, bsz, q, x_blk, b_blk):
    inner = GROUPS * HPG * HEADDIM
    return pl.pallas_call(
        _ctx_state_kernel,
        grid=(bsz,),
        in_specs=[pl.BlockSpec((q, inner), lambda b: (b, x_blk)),
                  pl.BlockSpec((q, GROUPS * STATE), lambda b: (b, b_blk)),
                  pl.BlockSpec((q, LANES), lambda b: (b, 0)),
                  pl.BlockSpec((1, LANES), lambda b: (0, 0))],
        out_specs=pl.BlockSpec((2, None, GROUPS, STATE, HPG * HEADDIM), lambda b: (0, b, 0, 0, 0)),
        out_shape=jax.ShapeDtypeStruct((2, bsz, GROUPS, STATE, HPG * HEADDIM), F32),
        scratch_shapes=[pltpu.VMEM((GROUPS, q, GL), F32), pltpu.VMEM((GROUPS, GL, q), F32),
                        pltpu.VMEM((GROUPS, GL, q), F32)],
        compiler_params=_cparams(("arbitrary",)),
        name="ctx_state",
    )(pc, pc, dtc, alog_row)


def _ssd_kernel(xf_ref, bf_ref, cf_ref, dtf_ref, xb_ref, bb_ref, cb_ref, dtb_ref, alog_ref, dskip_ref, h0_ref,
                yf_ref, yb_ref, s_scr, cs_scr, cst_scr, dtt_scr):
    q = xf_ref.shape[0]

    @pl.when(pl.program_id(1) == 0)
    def _():
        s_scr[...] = h0_ref[...]

    lane_f = (lax.broadcasted_iota(jnp.int32, (q, LANES), 1) % GL) < HPG
    dt = jnp.where(lane_f, dtf_ref[...], dtb_ref[...])
    _decay_tables(dt, alog_ref[...], cs_scr, cst_scr, dtt_scr)

    ri = lax.broadcasted_iota(jnp.int32, (q, q), 0)
    ci = lax.broadcasted_iota(jnp.int32, (q, q), 1)
    lane_lo = lax.broadcasted_iota(jnp.int32, (q, LANES), 1) < HEADDIM
    lane_lo_row = lax.broadcasted_iota(jnp.int32, (1, LANES), 1) < HEADDIM
    dirs = ((xf_ref, bf_ref, cf_ref, yf_ref), (xb_ref, bb_ref, cb_ref, yb_ref))

    def group_body(g, carry):
        cs_g, cst_g, dtt_g = cs_scr[g], cst_scr[g], dtt_scr[g]
        for d, (x_ref, b_ref, c_ref, y_ref) in enumerate(dirs):
            last = q - 1 if d == 0 else 0
            mask = (ri >= ci) if d == 0 else (ri <= ci)
            gcol = pl.ds(pl.multiple_of(g * STATE, STATE), STATE)
            bg = b_ref[:, gcol]
            cg = c_ref[:, gcol]
            cb = lax.dot_general(cg, bg, (((1,), (1,)), ((), ())), preferred_element_type=F32)
            bg_t = bg.astype(F32).T
            s_g = s_scr[d, g]
            y_off = _dot(cg, s_g.astype(BF16))
            for pair in range(HPG // 2):
                col0 = pl.multiple_of(g * (HPG * HEADDIM) + pair * LANES, LANES)
                xp = x_ref[:, pl.ds(col0, LANES)]
                bd = _block_diag_pair(xp)
                ms, bts, ecols, elasts = [], [], [], []
                for k in range(2):
                    colb, rowb, dtrow, clast = _head_terms(cs_g, cst_g, dtt_g, d * HPG + pair * 2 + k, last)
                    decay = jnp.exp(jnp.where(mask, colb - rowb, -jnp.inf))
                    ms.append((cb * decay * dtrow).astype(BF16))
                    bts.append((bg_t * (jnp.exp(clast - rowb) * dtrow)).astype(BF16))
                    ecols.append(jnp.exp(colb))
                    elasts.append(jnp.exp(clast))
                psl = slice(pair * LANES, (pair + 1) * LANES)
                y = _dot(jnp.concatenate(ms, axis=1), bd)
                y = y + y_off[:, psl] * jnp.where(lane_lo, ecols[0], ecols[1])
                if d == 0:
                    y = y + xp.astype(F32) * dskip_ref[:, pl.ds(col0, LANES)]
                y_ref[:, pl.ds(col0, LANES)] = y.astype(BF16)
                grow = jnp.where(lane_lo_row, elasts[0], elasts[1])
                s_scr[d, g, :, psl] = s_g[:, psl] * grow + _dot(jnp.concatenate(bts, axis=1), bd)
        return carry

    lax.fori_loop(0, GROUPS, group_body, 0)


def _ssd(p, dt, alog_row, dskip_row, h0, *, bsz, nc, x_blk, b_blk, c_blk):
    q = SSD_CHUNK
    t = p.shape[0]
    inner = GROUPS * HPG * HEADDIM
    gs = GROUPS * STATE

    def fwd(b, c):
        return b * nc + c

    def bwd(b, c):
        return b * nc + (nc - 1 - c)

    def specs(row):
        return [pl.BlockSpec((q, inner), lambda b, c: (row(b, c), x_blk)),
                pl.BlockSpec((q, gs), lambda b, c: (row(b, c), b_blk)),
                pl.BlockSpec((q, gs), lambda b, c: (row(b, c), c_blk)),
                pl.BlockSpec((q, LANES), lambda b, c: (row(b, c), 0))]

    return pl.pallas_call(
        _ssd_kernel,
        grid=(bsz, nc),
        in_specs=specs(fwd) + specs(bwd) + [
            pl.BlockSpec((1, LANES), lambda b, c: (0, 0)),
            pl.BlockSpec((1, inner), lambda b, c: (0, 0)),
            pl.BlockSpec((2, None, GROUPS, STATE, HPG * HEADDIM), lambda b, c: (0, b, 0, 0, 0))],
        out_specs=[pl.BlockSpec((q, inner), lambda b, c: (fwd(b, c), 0)),
                   pl.BlockSpec((q, inner), lambda b, c: (bwd(b, c), 0))],
        out_shape=[jax.ShapeDtypeStruct((t, inner), BF16), jax.ShapeDtypeStruct((t, inner), BF16)],
        scratch_shapes=[pltpu.VMEM((2, GROUPS, STATE, HPG * HEADDIM), F32),
                        pltpu.VMEM((GROUPS, q, GL), F32), pltpu.VMEM((GROUPS, GL, q), F32),
                        pltpu.VMEM((GROUPS, GL, q), F32)],
        compiler_params=_cparams(("arbitrary", "arbitrary")),
        name="ssd",
    )(p, p, p, dt, p, p, p, dt, alog_row, dskip_row, h0)


def _gnorm_kernel(yf_ref, yb_ref, z_ref, w_ref, o_ref):
    y = (yf_ref[...].astype(F32) + yb_ref[...].astype(F32)) * jax.nn.silu(z_ref[...].astype(F32))
    gw = y.shape[1] // GROUPS
    for g in range(GROUPS):
        yg = y[:, g * gw:(g + 1) * gw]
        ms = jnp.mean(yg * yg, axis=-1, keepdims=True)
        o_ref[:, g * gw:(g + 1) * gw] = (yg * lax.rsqrt(ms + EPS) * w_ref[:, g * gw:(g + 1) * gw]).astype(BF16)


def _gnorm(yf, yb, p, z_blk, norm_w, *, tm):
    t, inner = yf.shape
    return pl.pallas_call(
        _gnorm_kernel,
        grid=(t // tm,),
        in_specs=[pl.BlockSpec((tm, inner), lambda i: (i, 0)),
                  pl.BlockSpec((tm, inner), lambda i: (i, 0)),
                  pl.BlockSpec((tm, inner), lambda i: (i, z_blk)),
                  pl.BlockSpec((1, inner), lambda i: (0, 0))],
        out_specs=pl.BlockSpec((tm, inner), lambda i: (i, 0)),
        out_shape=jax.ShapeDtypeStruct((t, inner), BF16),
        compiler_params=_cparams(("parallel",)),
        name="gnorm",
    )(yf, yb, p, norm_w.reshape(1, inner))


def _merge_kernel(us_ref, yn_ref, ga_ref, gb_ref, wa_ref, wb_ref, o_ref):
    for s in range(o_ref.shape[1] // SUB):
        cols = slice(s * SUB, (s + 1) * SUB)
        ya = _dot(us_ref[...], wa_ref[:, cols])
        yb = _dot(yn_ref[...], wb_ref[:, cols])
        o_ref[:, cols] = (ga_ref[:, cols].astype(F32) * ya + gb_ref[:, cols].astype(F32) * yb).astype(BF16)


def _merge(us, yn, p, gate_col0, wa, wb, *, tm, tn):
    t = us.shape[0]
    wa_in, d = wa.shape
    wb_in = wb.shape[0]
    ga_blk = gate_col0 // tn
    gb_blk = (gate_col0 + d) // tn
    return pl.pallas_call(
        _merge_kernel,
        grid=(t // tm, d // tn),
        in_specs=[pl.BlockSpec((tm, wa_in), lambda i, j: (i, 0)),
                  pl.BlockSpec((tm, wb_in), lambda i, j: (i, 0)),
                  pl.BlockSpec((tm, tn), lambda i, j: (i, ga_blk + j)),
                  pl.BlockSpec((tm, tn), lambda i, j: (i, gb_blk + j)),
                  pl.BlockSpec((wa_in, tn), lambda i, j: (0, j)),
                  pl.BlockSpec((wb_in, tn), lambda i, j: (0, j))],
        out_specs=pl.BlockSpec((tm, tn), lambda i, j: (i, j)),
        out_shape=jax.ShapeDtypeStruct((t, d), BF16),
        compiler_params=_cparams(("parallel", "arbitrary")),
        name="merge",
    )(us, yn, p, p, wa, wb)


def _outproj_kernel(mg_ref, x_ref, m_ref, wo_ref, o_ref):
    for s in range(o_ref.shape[1] // OUT_SUB):
        cols = slice(s * OUT_SUB, (s + 1) * OUT_SUB)
        o_ref[:, cols] = x_ref[:, cols] + m_ref[:, cols] * _dot(mg_ref[...], wo_ref[:, cols])


def _outproj(mg, x2, mod3, mod_row, wo, *, tm):
    t, d = x2.shape
    return pl.pallas_call(
        _outproj_kernel,
        grid=(t // tm,),
        in_specs=[pl.BlockSpec((tm, d), lambda i: (i, 0)),
                  pl.BlockSpec((tm, d), lambda i: (i, 0)),
                  pl.BlockSpec((None, 1, d), lambda i: (mod_row(i), 0, 5)),
                  pl.BlockSpec((d, d), lambda i: (0, 0))],
        out_specs=pl.BlockSpec((tm, d), lambda i: (i, 0)),
        out_shape=jax.ShapeDtypeStruct((t, d), F32),
        compiler_params=_cparams(("parallel",)),
        name="outproj",
    )(mg, x2, mod3, wo)


def _dt_perm(a):
    lead = a.shape[:-2]
    a = a.reshape(*lead, 2, GROUPS, HPG)
    a = jnp.swapaxes(a, -3, -2)
    return a.reshape(*lead, 2 * GROUPS * HPG)


def kernel(x, c, ctx, c_ctx, w_mod, b_mod, norm_ffn1, ffn1_gate, ffn1_up, ffn1_down, norm_mix, w_in, b_gate,
           gmlp_ln_w, gmlp_ln_b, gmlp_ws, gmlp_bs, w_a, conv_w, conv_b, a_log, dt_bias, d_skip, ssm_norm,
           w_b, w_out, norm_ffn2, ffn2_gate, ffn2_up, ffn2_down, norm_final):
    bsz, seq, d = x.shape
    ctx_len = ctx.shape[1]
    depth = w_mod.shape[0]
    assert depth == 1, "context stream update between layers is not implemented"
    assert bsz < 8
    inner = GROUPS * HPG * HEADDIM
    gs = GROUPS * STATE
    a_width = w_a.shape[1]
    n_heads = GROUPS * HPG
    assert w_b.shape[1] == inner and conv_w.shape[2] == inner + 2 * gs
    assert w_in.shape[2] == 2 * a_width + 2 * inner + 2 * gs + 2 * n_heads + 2 * d

    t = bsz * seq
    tc = bsz * ctx_len
    x2 = x.reshape(t, d)
    ctx2 = ctx.reshape(tc, d)

    off_dt = 2 * a_width + 2 * inner + 2 * gs
    off_gate = off_dt + 2 * n_heads
    n_gelu, n_plain, n_conv = 2 * a_width, inner, inner + 2 * gs
    col_z = n_gelu
    col_x = n_gelu + n_plain
    col_b = col_x + inner
    col_c = col_b + gs
    col_gate = n_gelu + n_plain + n_conv

    i = 0
    w_main = jnp.concatenate([w_in[i][:, :off_dt], w_in[i][:, off_gate:]], axis=1).astype(BF16)
    w_dt = _dt_perm(w_in[i][:, off_dt:off_gate].reshape(d, 2, n_heads)).astype(BF16)
    dt_bias_row = _dt_perm(dt_bias[i]).reshape(1, 2 * n_heads)
    alog_row = _dt_perm(a_log[i].astype(F32)).reshape(1, 2 * n_heads)
    dskip_row = jnp.repeat(d_skip[i], HEADDIM).reshape(1, inner)
    bf = lambda w: w.astype(BF16)

    c8 = jnp.zeros((8, d), F32).at[:bsz].set(c).at[bsz].set(c_ctx)
    mod3 = _adaln(c8, w_mod[i], b_mod[i]).reshape(8, 1, N_MOD * d)

    tm = 512
    lat_row = lambda ti: (ti * tm) // seq
    ctx_row = lambda ti: bsz

    ffn1_w = (bf(ffn1_gate[i]), bf(ffn1_up[i]), bf(ffn1_down[i]))
    x1 = _ffn(x2, mod3, 0, lat_row, norm_ffn1[i], *ffn1_w, None, tm=tm, tf=512)
    ctx1 = _ffn(ctx2, mod3, 0, ctx_row, norm_ffn1[i], *ffn1_w, None, tm=tm, tf=512)

    proj_args = (norm_mix[i], w_main, w_dt, dt_bias_row, conv_w[i], conv_b[i].reshape(1, -1), b_gate[i].reshape(1, -1))
    proj_kw = dict(tm=tm, tn=1024, n_gelu=n_gelu, n_plain=n_plain, n_conv=n_conv)
    pc, dtc = _inproj(ctx1, mod3, ctx_row, *proj_args, seg=ctx_len, **proj_kw)
    h0 = _ctx_states(pc, dtc, alog_row, bsz=bsz, q=ctx_len, x_blk=col_x // inner, b_blk=col_b // gs)

    p, dt = _inproj(x1, mod3, lat_row, *proj_args, seg=GRID_W, **proj_kw)
    us = _gmlp(p, a_width, gmlp_ln_w[i], gmlp_ln_b[i], bf(gmlp_ws[i]), gmlp_bs[i].T, tm=tm)
    yf, yb = _ssd(p, dt, alog_row, dskip_row, h0, bsz=bsz, nc=seq // SSD_CHUNK,
                  x_blk=col_x // inner, b_blk=col_b // gs, c_blk=col_c // gs)
    yn = _gnorm(yf, yb, p, col_z // inner, ssm_norm[i], tm=256)
    mg = _merge(us, yn, p, col_gate, bf(w_a[i]), bf(w_b[i]), tm=1024, tn=512)
    x3 = _outproj(mg, x1, mod3, lat_row, bf(w_out[i]), tm=tm)

    out = _ffn(x3, mod3, 6, lat_row, norm_ffn2[i], bf(ffn2_gate[i]), bf(ffn2_up[i]), bf(ffn2_down[i]),
               norm_final, tm=tm, tf=512)
    return out.reshape(bsz, seq, d)
```

```python
import functools

import jax
import jax.numpy as jnp
from jax import lax
from jax.experimental import pallas as pl
from jax.experimental.pallas import tpu as pltpu

F32 = jnp.float32
BF16 = jnp.bfloat16
EPS = 1e-6
LOG2E = 1.4426950408889634

GRID_W = 64
A_GROUPS = 8
A_CHUNK = 128
HEADDIM = 64
GROUPS = 8
STATE = 128
CONV_K = 5
SSD_CHUNK = 128
N_MOD = 9

LANES = 128
SUB = 256
OUT_SUB = 512
PROJ_SUB = 512
SUBLANES = 8
VMEM_LIMIT = 56 * 1024 * 1024


def _cparams(sem):
    return pltpu.CompilerParams(dimension_semantics=sem, vmem_limit_bytes=VMEM_LIMIT)


def _dot(a, b):
    return jnp.dot(a, b, preferred_element_type=F32)


def _split3(a):
    a1 = a.astype(BF16)
    r1 = a - a1.astype(F32)
    a2 = r1.astype(BF16)
    r2 = r1 - a2.astype(F32)
    return a1, a2, r2.astype(BF16)


def _dot_exact_lhs(tri, a):
    a1, a2, a3 = _split3(a)
    return _dot(tri, a1) + _dot(tri, a2) + _dot(tri, a3)


def _dot_exact_rhs(a, tri):
    a1, a2, a3 = _split3(a)
    return _dot(a1, tri) + _dot(a2, tri) + _dot(a3, tri)


def _rms_mod(x, nw, scale, shift):
    ms = jnp.mean(x * x, axis=-1, keepdims=True)
    y = x * lax.rsqrt(ms + EPS) * nw
    return y * (1.0 + scale) + shift


def _adaln_kernel(c_ref, w_ref, b_ref, o_ref):
    s = jax.nn.silu(c_ref[...])
    s1, s2, s3 = _split3(s)
    w = w_ref[...].astype(BF16)
    o_ref[...] = _dot(s1, w) + _dot(s2, w) + _dot(s3, w) + b_ref[...]


def _adaln(c8, w_mod, b_mod):
    d, n = w_mod.shape
    tn = 1024
    return pl.pallas_call(
        _adaln_kernel,
        grid=(n // tn,),
        in_specs=[pl.BlockSpec((8, d), lambda j: (0, 0)),
                  pl.BlockSpec((d, tn), lambda j: (0, j)),
                  pl.BlockSpec((1, tn), lambda j: (0, j))],
        out_specs=pl.BlockSpec((8, tn), lambda j: (0, j)),
        out_shape=jax.ShapeDtypeStruct((8, n), F32),
        compiler_params=_cparams(("arbitrary",)),
        name="adaln",
    )(c8, w_mod, b_mod.reshape(1, n))


def _ffn_kernel(x_ref, sh_ref, sc_ref, gt_ref, nw_ref, wg_ref, wu_ref, wd_ref, fw_ref, o_ref,
                h_scr, acc_scr, *, final_norm):
    j = pl.program_id(1)

    @pl.when(j == 0)
    def _():
        h_scr[...] = _rms_mod(x_ref[...], nw_ref[...], sc_ref[...], sh_ref[...]).astype(BF16)
        acc_scr[...] = jnp.zeros_like(acc_scr)

    acts = []
    for s in range(wg_ref.shape[1] // SUB):
        cols = slice(s * SUB, (s + 1) * SUB)
        g = _dot(h_scr[...], wg_ref[:, cols])
        u = _dot(h_scr[...], wu_ref[:, cols])
        acts.append((jax.nn.silu(g) * u).astype(BF16))
    acc_scr[...] += _dot(jnp.concatenate(acts, axis=1), wd_ref[...])

    @pl.when(j == pl.num_programs(1) - 1)
    def _():
        out = x_ref[...] + 0.5 * gt_ref[...] * acc_scr[...]
        if final_norm:
            ms = jnp.mean(out * out, axis=-1, keepdims=True)
            out = out * lax.rsqrt(ms + EPS) * fw_ref[...]
        o_ref[...] = out


def _ffn(x2, mod3, mod_k, mod_row, norm_w, wg, wu, wd, final_w, *, tm, tf):
    t, d = x2.shape
    f = wg.shape[1]
    final_norm = final_w is not None
    fw = final_w if final_norm else norm_w

    def mod_spec(k):
        return pl.BlockSpec((None, 1, d), lambda i, j: (mod_row(i), 0, k))

    return pl.pallas_call(
        functools.partial(_ffn_kernel, final_norm=final_norm),
        grid=(t // tm, f // tf),
        in_specs=[pl.BlockSpec((tm, d), lambda i, j: (i, 0)),
                  mod_spec(mod_k), mod_spec(mod_k + 1), mod_spec(mod_k + 2),
                  pl.BlockSpec((1, d), lambda i, j: (0, 0)),
                  pl.BlockSpec((d, tf), lambda i, j: (0, j)),
                  pl.BlockSpec((d, tf), lambda i, j: (0, j)),
                  pl.BlockSpec((tf, d), lambda i, j: (j, 0)),
                  pl.BlockSpec((1, d), lambda i, j: (0, 0))],
        out_specs=pl.BlockSpec((tm, d), lambda i, j: (i, 0)),
        out_shape=jax.ShapeDtypeStruct((t, d), F32),
        scratch_shapes=[pltpu.VMEM((tm, d), BF16), pltpu.VMEM((tm, d), F32)],
        compiler_params=_cparams(("parallel", "arbitrary")),
        name="ffn",
    )(x2, mod3, mod3, mod3, norm_w.reshape(1, d), wg, wu, wd, fw.reshape(1, d))


def _shift_in_segment(x4, off):
    nseg, _, sub, c = x4.shape
    r = pltpu.roll(x4, (-off) % sub, axis=2)
    zero = jnp.zeros((nseg, 1, sub, c), x4.dtype)
    row = lax.broadcasted_iota(jnp.int32, (1, 1, sub, c), 2)
    if off < 0:
        return jnp.where(row < -off, jnp.concatenate([zero, r[:, :-1]], axis=1), r)
    return jnp.where(row >= sub - off, jnp.concatenate([r[:, 1:], zero], axis=1), r)


def _inproj_kernel(x_ref, sh_ref, sc_ref, nw_ref, w_ref, wdt_ref, dtb_ref, cw_ref, cb_ref, bg_ref,
                   p_ref, dt_ref, h_scr, acc_scr, *, seg, j_gelu, j_plain, j_conv):
    j = pl.program_id(1)
    tm, tn = p_ref.shape

    @pl.when(j == 0)
    def _():
        h = _rms_mod(x_ref[...], nw_ref[...], sc_ref[...], sh_ref[...]).astype(BF16)
        h_scr[...] = h
        dt_ref[...] = jax.nn.softplus(_dot(h, wdt_ref[...]) + dtb_ref[...])

    def run(epilogue):
        cols = [slice(s * PROJ_SUB, (s + 1) * PROJ_SUB) for s in range(tn // PROJ_SUB)]
        for s, c in enumerate(cols):
            if s > 0:
                p_ref[:, cols[s - 1]] = epilogue(acc_scr[(s - 1) % 2], cols[s - 1]).astype(BF16)
            acc_scr[s % 2] = _dot(h_scr[...], w_ref[:, c])
        last = len(cols) - 1
        p_ref[:, cols[last]] = epilogue(acc_scr[last % 2], cols[last]).astype(BF16)

    @pl.when(j < j_gelu)
    def _():
        run(lambda acc, cols: jax.nn.gelu(acc))

    @pl.when((j >= j_gelu) & (j < j_plain))
    def _():
        run(lambda acc, cols: acc)

    @pl.when((j >= j_plain) & (j < j_conv))
    def _():
        pad = (CONV_K - 1) // 2

        def conv_silu(acc, cols):
            x4 = acc.reshape(tm // seg, seg // SUBLANES, SUBLANES, PROJ_SUB)
            y = cb_ref[:, cols] + cw_ref[pad:pad + 1, cols] * x4
            for tap in range(CONV_K):
                if tap != pad:
                    y = y + cw_ref[tap:tap + 1, cols] * _shift_in_segment(x4, tap - pad)
            return jax.nn.silu(y).reshape(tm, PROJ_SUB)

        run(conv_silu)

    @pl.when(j >= j_conv)
    def _():
        run(lambda acc, cols: jax.nn.sigmoid(acc + bg_ref[:, cols]))


def _inproj(x2, mod3, mod_row, norm_w, w_main, w_dt, dt_bias, conv_w, conv_b, b_gate, *, seg, tm, tn,
            n_gelu, n_plain, n_conv):
    t, d = x2.shape
    n = w_main.shape[1]
    j_gelu = n_gelu // tn
    j_plain = j_gelu + n_plain // tn
    j_conv = j_plain + n_conv // tn
    nconv_t = n_conv // tn
    ngate_t = b_gate.shape[1] // tn

    def mod_spec(k):
        return pl.BlockSpec((None, 1, d), lambda i, j: (mod_row(i), 0, k))

    return pl.pallas_call(
        functools.partial(_inproj_kernel, seg=seg, j_gelu=j_gelu, j_plain=j_plain, j_conv=j_conv),
        grid=(t // tm, n // tn),
        in_specs=[pl.BlockSpec((tm, d), lambda i, j: (i, 0)),
                  mod_spec(3), mod_spec(4),
                  pl.BlockSpec((1, d), lambda i, j: (0, 0)),
                  pl.BlockSpec((d, tn), lambda i, j: (0, j)),
                  pl.BlockSpec((d, LANES), lambda i, j: (0, 0)),
                  pl.BlockSpec((1, LANES), lambda i, j: (0, 0)),
                  pl.BlockSpec((CONV_K, tn), lambda i, j: (0, jnp.clip(j - j_plain, 0, nconv_t - 1))),
                  pl.BlockSpec((1, tn), lambda i, j: (0, jnp.clip(j - j_plain, 0, nconv_t - 1))),
                  pl.BlockSpec((1, tn), lambda i, j: (0, jnp.clip(j - j_conv, 0, ngate_t - 1)))],
        out_specs=[pl.BlockSpec((tm, tn), lambda i, j: (i, j)),
                   pl.BlockSpec((tm, LANES), lambda i, j: (i, 0))],
        out_shape=[jax.ShapeDtypeStruct((t, n), BF16), jax.ShapeDtypeStruct((t, LANES), F32)],
        scratch_shapes=[pltpu.VMEM((tm, d), BF16), pltpu.VMEM((2, tm, PROJ_SUB), F32)],
        compiler_params=_cparams(("parallel", "arbitrary")),
        name="inproj",
    )(x2, mod3, mod3, norm_w.reshape(1, d), w_main, w_dt, dt_bias, conv_w, conv_b, b_gate)


def _gmlp_kernel(u_ref, v_ref, lw_ref, lb_ref, ws_ref, bs_ref, o_ref):
    v = v_ref[...].astype(F32)
    mu = jnp.mean(v, axis=-1, keepdims=True)
    vc = v - mu
    var = jnp.mean(vc * vc, axis=-1, keepdims=True)
    vn = (vc * lax.rsqrt(var + EPS) * lw_ref[...] + lb_ref[...]).astype(BF16)
    tm, width = vn.shape
    gw = width // A_GROUPS
    for c in range(tm // A_CHUNK):
        r0 = c * A_CHUNK
        for g in range(A_GROUPS):
            c0 = g * gw
            s = _dot(ws_ref[g], vn[r0:r0 + A_CHUNK, c0:c0 + gw]) + bs_ref[:, g:g + 1]
            u = u_ref[r0:r0 + A_CHUNK, c0:c0 + gw].astype(F32)
            o_ref[r0:r0 + A_CHUNK, c0:c0 + gw] = (u * s).astype(BF16)


def _gmlp(p, width, ln_w, ln_b, ws, bs_t, *, tm):
    t = p.shape[0]
    return pl.pallas_call(
        _gmlp_kernel,
        grid=(t // tm,),
        in_specs=[pl.BlockSpec((tm, width), lambda i: (i, 0)),
                  pl.BlockSpec((tm, width), lambda i: (i, 1)),
                  pl.BlockSpec((1, width), lambda i: (0, 0)),
                  pl.BlockSpec((1, width), lambda i: (0, 0)),
                  pl.BlockSpec((A_GROUPS, A_CHUNK, A_CHUNK), lambda i: (0, 0, 0)),
                  pl.BlockSpec((A_CHUNK, A_GROUPS), lambda i: (0, 0))],
        out_specs=pl.BlockSpec((tm, width), lambda i: (i, 0)),
        out_shape=jax.ShapeDtypeStruct((t, width), BF16),
        compiler_params=_cparams(("parallel",)),
        name="gmlp",
    )(p, p, ln_w.reshape(1, width), ln_b.reshape(1, width), ws, bs_t)


HPG = 8
GL = 2 * HPG


def _decay_tables(dt, alog_row, cs_scr, ecs_scr, rowt_scr):
    q = dt.shape[0]
    a = dt * (-jnp.exp(alog_row) * LOG2E)
    ri = lax.broadcasted_iota(jnp.int32, (q, q), 0)
    ci = lax.broadcasted_iota(jnp.int32, (q, q), 1)
    tril = jnp.where(ri >= ci, 1.0, 0.0).astype(BF16)
    triu = jnp.where(ri <= ci, 1.0, 0.0).astype(BF16)
    lane_f = (lax.broadcasted_iota(jnp.int32, (q, LANES), 1) % GL) < HPG
    sub_f = (lax.broadcasted_iota(jnp.int32, (LANES, q), 0) % GL) < HPG
    cs = jnp.where(lane_f, _dot_exact_lhs(tril, a), _dot_exact_lhs(triu, a))
    a_t = a.T
    cst = jnp.where(sub_f, _dot_exact_rhs(a_t, triu), _dot_exact_rhs(a_t, tril))
    rowt = cst - jnp.log2(dt.T)
    ecs = jnp.exp2(cs)
    for g in range(GROUPS):
        cs_scr[g] = cs[:, g * GL:(g + 1) * GL]
        ecs_scr[g] = ecs[:, g * GL:(g + 1) * GL]
        rowt_scr[g] = rowt[g * GL:(g + 1) * GL, :]


def _pair_masks(q):
    lane = lax.broadcasted_iota(jnp.int32, (q, LANES), 1)
    return (jnp.where(lane < HEADDIM, 1.0, 0.0).astype(BF16),
            jnp.where(lane >= HEADDIM, 1.0, 0.0).astype(BF16))


def _block_diag_pair(xp, masks):
    return jnp.concatenate([xp * masks[0], xp * masks[1]], axis=0)


def _head_terms(cs_g, rowt_g, hl, last):
    q = cs_g.shape[0]
    col = cs_g[:, hl:hl + 1]
    rowt = rowt_g[hl:hl + 1, :]
    clast = jnp.broadcast_to(col[last:last + 1, :], (1, q))
    return col, rowt, clast


def _ctx_state_kernel(x_ref, b_ref, dt_ref, alog_ref, h_ref, cs_scr, ecs_scr, rowt_scr):
    q = x_ref.shape[0]
    _decay_tables(dt_ref[...], alog_ref[...], cs_scr, ecs_scr, rowt_scr)
    masks = _pair_masks(q)

    def group_body(g, carry):
        cs_g, rowt_g = cs_scr[g], rowt_scr[g]
        bg = b_ref[:, pl.ds(pl.multiple_of(g * STATE, STATE), STATE)]
        bg_t = bg.astype(F32).T.astype(BF16)
        for d in range(2):
            last = q - 1 if d == 0 else 0
            for pair in range(HPG // 2):
                col0 = pl.multiple_of(g * (HPG * HEADDIM) + pair * LANES, LANES)
                bd = _block_diag_pair(x_ref[:, pl.ds(col0, LANES)], masks)
                bts = []
                for k in range(2):
                    _, rowt, clast = _head_terms(cs_g, rowt_g, d * HPG + pair * 2 + k, last)
                    bts.append(bg_t * jnp.exp2(clast - rowt).astype(BF16))
                h_ref[d, g, :, pair * LANES:(pair + 1) * LANES] = _dot(jnp.concatenate(bts, axis=1), bd)
        return carry

    lax.fori_loop(0, GROUPS, group_body, 0)


def _ctx_states(pc, dtc, alog_row, *, bsz, q, x_blk, b_blk):
    inner = GROUPS * HPG * HEADDIM
    return pl.pallas_call(
        _ctx_state_kernel,
        grid=(bsz,),
        in_specs=[pl.BlockSpec((q, inner), lambda b: (b, x_blk)),
                  pl.BlockSpec((q, GROUPS * STATE), lambda b: (b, b_blk)),
                  pl.BlockSpec((q, LANES), lambda b: (b, 0)),
                  pl.BlockSpec((1, LANES), lambda b: (0, 0))],
        out_specs=pl.BlockSpec((2, None, GROUPS, STATE, HPG * HEADDIM), lambda b: (0, b, 0, 0, 0)),
        out_shape=jax.ShapeDtypeStruct((2, bsz, GROUPS, STATE, HPG * HEADDIM), F32),
        scratch_shapes=[pltpu.VMEM((GROUPS, q, GL), F32), pltpu.VMEM((GROUPS, q, GL), F32),
                        pltpu.VMEM((GROUPS, GL, q), F32)],
        compiler_params=_cparams(("arbitrary",)),
        name="ctx_state",
    )(pc, pc, dtc, alog_row)


def _ssd_kernel(xf_ref, bf_ref, cf_ref, dtf_ref, xb_ref, bb_ref, cb_ref, dtb_ref, alog_ref, dskip_ref, h0_ref,
                yf_ref, yb_ref, s_scr, cs_scr, ecs_scr, rowt_scr):
    q = xf_ref.shape[0]

    @pl.when(pl.program_id(1) == 0)
    def _():
        s_scr[...] = h0_ref[...]

    lane_f = (lax.broadcasted_iota(jnp.int32, (q, LANES), 1) % GL) < HPG
    dt = jnp.where(lane_f, dtf_ref[...], dtb_ref[...])
    _decay_tables(dt, alog_ref[...], cs_scr, ecs_scr, rowt_scr)

    ri = lax.broadcasted_iota(jnp.int32, (q, q), 0)
    ci = lax.broadcasted_iota(jnp.int32, (q, q), 1)
    lane_lo = lax.broadcasted_iota(jnp.int32, (q, LANES), 1) < HEADDIM
    lane_lo_row = lax.broadcasted_iota(jnp.int32, (1, LANES), 1) < HEADDIM
    masks = _pair_masks(q)
    dirs = ((xf_ref, bf_ref, cf_ref, yf_ref), (xb_ref, bb_ref, cb_ref, yb_ref))

    def group_body(g, carry):
        cs_g, ecs_g, rowt_g = cs_scr[g], ecs_scr[g], rowt_scr[g]
        for d, (x_ref, b_ref, c_ref, y_ref) in enumerate(dirs):
            last = q - 1 if d == 0 else 0
            mask = (ri >= ci) if d == 0 else (ri <= ci)
            gcol = pl.ds(pl.multiple_of(g * STATE, STATE), STATE)
            bg = b_ref[:, gcol]
            cg = c_ref[:, gcol]
            cb = lax.dot_general(cg, bg, (((1,), (1,)), ((), ())), preferred_element_type=F32)
            bg_t = bg.astype(F32).T.astype(BF16)
            s_g = s_scr[d, g]
            y_off = _dot(cg, s_g.astype(BF16))
            for pair in range(HPG // 2):
                col0 = pl.multiple_of(g * (HPG * HEADDIM) + pair * LANES, LANES)
                xp = x_ref[:, pl.ds(col0, LANES)]
                bd = _block_diag_pair(xp, masks)
                ms, bts, ecols, elasts = [], [], [], []
                for k in range(2):
                    hl = d * HPG + pair * 2 + k
                    col, rowt, clast = _head_terms(cs_g, rowt_g, hl, last)
                    colb = jnp.broadcast_to(col, (q, q))
                    ms.append((cb * jnp.exp2(jnp.where(mask, colb - rowt, -jnp.inf))).astype(BF16))
                    bts.append(bg_t * jnp.exp2(clast - rowt).astype(BF16))
                    ecols.append(jnp.exp2(colb))
                    elasts.append(jnp.exp2(clast))
                psl = slice(pair * LANES, (pair + 1) * LANES)
                y = _dot(jnp.concatenate(ms, axis=1), bd)
                y = y + y_off[:, psl] * jnp.where(lane_lo, ecols[0], ecols[1])
                if d == 0:
                    y = y + xp.astype(F32) * dskip_ref[:, pl.ds(col0, LANES)]
                y_ref[:, pl.ds(col0, LANES)] = y.astype(BF16)
                grow = jnp.where(lane_lo_row, elasts[0], elasts[1])
                s_scr[d, g, :, psl] = s_g[:, psl] * grow + _dot(jnp.concatenate(bts, axis=1), bd)
        return carry

    lax.fori_loop(0, GROUPS, group_body, 0)


def _ssd(p, dt, alog_row, dskip_row, h0, *, bsz, nc, x_blk, b_blk, c_blk):
    q = SSD_CHUNK
    t = p.shape[0]
    inner = GROUPS * HPG * HEADDIM
    gs = GROUPS * STATE

    def fwd(b, c):
        return b * nc + c

    def bwd(b, c):
        return b * nc + (nc - 1 - c)

    def specs(row):
        return [pl.BlockSpec((q, inner), lambda b, c: (row(b, c), x_blk)),
                pl.BlockSpec((q, gs), lambda b, c: (row(b, c), b_blk)),
                pl.BlockSpec((q, gs), lambda b, c: (row(b, c), c_blk)),
                pl.BlockSpec((q, LANES), lambda b, c: (row(b, c), 0))]

    return pl.pallas_call(
        _ssd_kernel,
        grid=(bsz, nc),
        in_specs=specs(fwd) + specs(bwd) + [
            pl.BlockSpec((1, LANES), lambda b, c: (0, 0)),
            pl.BlockSpec((1, inner), lambda b, c: (0, 0)),
            pl.BlockSpec((2, None, GROUPS, STATE, HPG * HEADDIM), lambda b, c: (0, b, 0, 0, 0))],
        out_specs=[pl.BlockSpec((q, inner), lambda b, c: (fwd(b, c), 0)),
                   pl.BlockSpec((q, inner), lambda b, c: (bwd(b, c), 0))],
        out_shape=[jax.ShapeDtypeStruct((t, inner), BF16), jax.ShapeDtypeStruct((t, inner), BF16)],
        scratch_shapes=[pltpu.VMEM((2, GROUPS, STATE, HPG * HEADDIM), F32),
                        pltpu.VMEM((GROUPS, q, GL), F32), pltpu.VMEM((GROUPS, q, GL), F32),
                        pltpu.VMEM((GROUPS, GL, q), F32)],
        compiler_params=_cparams(("arbitrary", "arbitrary")),
        name="ssd",
    )(p, p, p, dt, p, p, p, dt, alog_row, dskip_row, h0)


def _gnorm_kernel(yf_ref, yb_ref, z_ref, w_ref, o_ref):
    y = (yf_ref[...].astype(F32) + yb_ref[...].astype(F32)) * jax.nn.silu(z_ref[...].astype(F32))
    gw = y.shape[1] // GROUPS
    for g in range(GROUPS):
        yg = y[:, g * gw:(g + 1) * gw]
        ms = jnp.mean(yg * yg, axis=-1, keepdims=True)
        o_ref[:, g * gw:(g + 1) * gw] = (yg * lax.rsqrt(ms + EPS) * w_ref[:, g * gw:(g + 1) * gw]).astype(BF16)


def _gnorm(yf, yb, p, z_blk, norm_w, *, tm):
    t, inner = yf.shape
    return pl.pallas_call(
        _gnorm_kernel,
        grid=(t // tm,),
        in_specs=[pl.BlockSpec((tm, inner), lambda i: (i, 0)),
                  pl.BlockSpec((tm, inner), lambda i: (i, 0)),
                  pl.BlockSpec((tm, inner), lambda i: (i, z_blk)),
                  pl.BlockSpec((1, inner), lambda i: (0, 0))],
        out_specs=pl.BlockSpec((tm, inner), lambda i: (i, 0)),
        out_shape=jax.ShapeDtypeStruct((t, inner), BF16),
        compiler_params=_cparams(("parallel",)),
        name="gnorm",
    )(yf, yb, p, norm_w.reshape(1, inner))


def _merge_kernel(us_ref, yn_ref, ga_ref, gb_ref, wa_ref, wb_ref, o_ref):
    for s in range(o_ref.shape[1] // SUB):
        cols = slice(s * SUB, (s + 1) * SUB)
        ya = _dot(us_ref[...], wa_ref[:, cols])
        yb = _dot(yn_ref[...], wb_ref[:, cols])
        o_ref[:, cols] = (ga_ref[:, cols].astype(F32) * ya + gb_ref[:, cols].astype(F32) * yb).astype(BF16)


def _merge(us, yn, p, gate_col0, wa, wb, *, tm, tn):
    t = us.shape[0]
    wa_in, d = wa.shape
    wb_in = wb.shape[0]
    ga_blk = gate_col0 // tn
    gb_blk = (gate_col0 + d) // tn
    return pl.pallas_call(
        _merge_kernel,
        grid=(t // tm, d // tn),
        in_specs=[pl.BlockSpec((tm, wa_in), lambda i, j: (i, 0)),
                  pl.BlockSpec((tm, wb_in), lambda i, j: (i, 0)),
                  pl.BlockSpec((tm, tn), lambda i, j: (i, ga_blk + j)),
                  pl.BlockSpec((tm, tn), lambda i, j: (i, gb_blk + j)),
                  pl.BlockSpec((wa_in, tn), lambda i, j: (0, j)),
                  pl.BlockSpec((wb_in, tn), lambda i, j: (0, j))],
        out_specs=pl.BlockSpec((tm, tn), lambda i, j: (i, j)),
        out_shape=jax.ShapeDtypeStruct((t, d), BF16),
        compiler_params=_cparams(("parallel", "arbitrary")),
        name="merge",
    )(us, yn, p, p, wa, wb)


def _outproj_kernel(mg_ref, x_ref, m_ref, wo_ref, o_ref):
    for s in range(o_ref.shape[1] // OUT_SUB):
        cols = slice(s * OUT_SUB, (s + 1) * OUT_SUB)
        o_ref[:, cols] = x_ref[:, cols] + m_ref[:, cols] * _dot(mg_ref[...], wo_ref[:, cols])


def _outproj(mg, x2, mod3, mod_row, wo, *, tm):
    t, d = x2.shape
    return pl.pallas_call(
        _outproj_kernel,
        grid=(t // tm,),
        in_specs=[pl.BlockSpec((tm, d), lambda i: (i, 0)),
                  pl.BlockSpec((tm, d), lambda i: (i, 0)),
                  pl.BlockSpec((None, 1, d), lambda i: (mod_row(i), 0, 5)),
                  pl.BlockSpec((d, d), lambda i: (0, 0))],
        out_specs=pl.BlockSpec((tm, d), lambda i: (i, 0)),
        out_shape=jax.ShapeDtypeStruct((t, d), F32),
        compiler_params=_cparams(("parallel",)),
        name="outproj",
    )(mg, x2, mod3, wo)


def _dt_perm(a):
    lead = a.shape[:-2]
    a = a.reshape(*lead, 2, GROUPS, HPG)
    a = jnp.swapaxes(a, -3, -2)
    return a.reshape(*lead, 2 * GROUPS * HPG)


def kernel(x, c, ctx, c_ctx, w_mod, b_mod, norm_ffn1, ffn1_gate, ffn1_up, ffn1_down, norm_mix, w_in, b_gate,
           gmlp_ln_w, gmlp_ln_b, gmlp_ws, gmlp_bs, w_a, conv_w, conv_b, a_log, dt_bias, d_skip, ssm_norm,
           w_b, w_out, norm_ffn2, ffn2_gate, ffn2_up, ffn2_down, norm_final):
    bsz, seq, d = x.shape
    ctx_len = ctx.shape[1]
    depth = w_mod.shape[0]
    assert depth == 1, "context stream update between layers is not implemented"
    assert bsz < 8
    inner = GROUPS * HPG * HEADDIM
    gs = GROUPS * STATE
    a_width = w_a.shape[1]
    n_heads = GROUPS * HPG
    assert w_b.shape[1] == inner and conv_w.shape[2] == inner + 2 * gs
    assert w_in.shape[2] == 2 * a_width + 2 * inner + 2 * gs + 2 * n_heads + 2 * d

    t = bsz * seq
    tc = bsz * ctx_len
    x2 = x.reshape(t, d)
    ctx2 = ctx.reshape(tc, d)

    off_dt = 2 * a_width + 2 * inner + 2 * gs
    off_gate = off_dt + 2 * n_heads
    n_gelu, n_plain, n_conv = 2 * a_width, inner, inner + 2 * gs
    col_z = n_gelu
    col_x = n_gelu + n_plain
    col_b = col_x + inner
    col_c = col_b + gs
    col_gate = n_gelu + n_plain + n_conv

    i = 0
    w_main = jnp.concatenate([w_in[i][:, :off_dt], w_in[i][:, off_gate:]], axis=1).astype(BF16)
    w_dt = _dt_perm(w_in[i][:, off_dt:off_gate].reshape(d, 2, n_heads)).astype(BF16)
    dt_bias_row = _dt_perm(dt_bias[i]).reshape(1, 2 * n_heads)
    alog_row = _dt_perm(a_log[i].astype(F32)).reshape(1, 2 * n_heads)
    dskip_row = jnp.repeat(d_skip[i], HEADDIM).reshape(1, inner)
    bf = lambda w: w.astype(BF16)

    c8 = jnp.zeros((8, d), F32).at[:bsz].set(c).at[bsz].set(c_ctx)
    mod3 = _adaln(c8, w_mod[i], b_mod[i]).reshape(8, 1, N_MOD * d)

    tm = 512
    tm_proj = 512

    def lat_row(rows):
        assert seq % rows == 0, "a token tile must not straddle two samples"
        return lambda ti: (ti * rows) // seq

    ctx_row = lambda ti: bsz

    ffn1_w = (bf(ffn1_gate[i]), bf(ffn1_up[i]), bf(ffn1_down[i]))
    x1 = _ffn(x2, mod3, 0, lat_row(tm), norm_ffn1[i], *ffn1_w, None, tm=tm, tf=512)
    ctx1 = _ffn(ctx2, mod3, 0, ctx_row, norm_ffn1[i], *ffn1_w, None, tm=tm, tf=512)

    proj_args = (norm_mix[i], w_main, w_dt, dt_bias_row, conv_w[i], conv_b[i].reshape(1, -1), b_gate[i].reshape(1, -1))
    proj_kw = dict(tm=tm_proj, tn=2048, n_gelu=n_gelu, n_plain=n_plain, n_conv=n_conv)
    pc, dtc = _inproj(ctx1, mod3, ctx_row, *proj_args, seg=ctx_len, **proj_kw)
    h0 = _ctx_states(pc, dtc, alog_row, bsz=bsz, q=ctx_len, x_blk=col_x // inner, b_blk=col_b // gs)

    p, dt = _inproj(x1, mod3, lat_row(tm_proj), *proj_args, seg=GRID_W, **proj_kw)
    us = _gmlp(p, a_width, gmlp_ln_w[i], gmlp_ln_b[i], bf(gmlp_ws[i]), gmlp_bs[i].T, tm=tm)
    yf, yb = _ssd(p, dt, alog_row, dskip_row, h0, bsz=bsz, nc=seq // SSD_CHUNK,
                  x_blk=col_x // inner, b_blk=col_b // gs, c_blk=col_c // gs)
    yn = _gnorm(yf, yb, p, col_z // inner, ssm_norm[i], tm=256)
    mg = _merge(us, yn, p, col_gate, bf(w_a[i]), bf(w_b[i]), tm=1024, tn=512)
    x3 = _outproj(mg, x1, mod3, lat_row(tm), bf(w_out[i]), tm=tm)

    out = _ffn(x3, mod3, 6, lat_row(tm), norm_ffn2[i], bf(ffn2_gate[i]), bf(ffn2_up[i]), bf(ffn2_down[i]),
               norm_final, tm=tm, tf=512)
    return out.reshape(bsz, seq, d)
```

```python
import functools

import jax
import jax.numpy as jnp
from jax import lax
from jax.experimental import pallas as pl
from jax.experimental.pallas import tpu as pltpu

F32 = jnp.float32
BF16 = jnp.bfloat16
EPS = 1e-6
LOG2E = 1.4426950408889634

GRID_W = 64
A_GROUPS = 8
A_CHUNK = 128
HEADDIM = 64
GROUPS = 8
STATE = 128
CONV_K = 5
SSD_CHUNK = 128
N_MOD = 9

LANES = 128
SUB = 256
OUT_SUB = 512
PROJ_SUB = 512
SUBLANES = 8
BF16_ROWS = 16
VMEM_LIMIT = 56 * 1024 * 1024


def _cparams(sem):
    return pltpu.CompilerParams(dimension_semantics=sem, vmem_limit_bytes=VMEM_LIMIT)


def _dot(a, b):
    return jnp.dot(a, b, preferred_element_type=F32)


def _split3(a):
    a1 = a.astype(BF16)
    r1 = a - a1.astype(F32)
    a2 = r1.astype(BF16)
    r2 = r1 - a2.astype(F32)
    return a1, a2, r2.astype(BF16)


def _dot_exact_lhs(tri, a):
    a1, a2, a3 = _split3(a)
    return _dot(tri, a1) + _dot(tri, a2) + _dot(tri, a3)


def _dot_exact_rhs(a, tri):
    a1, a2, a3 = _split3(a)
    return _dot(a1, tri) + _dot(a2, tri) + _dot(a3, tri)


def _rms_mod_store(x_ref, h_ref, nw, scale, shift):
    gain = nw * (1.0 + scale)
    for r in range(0, x_ref.shape[0], BF16_ROWS):
        x = x_ref[r:r + BF16_ROWS, :]
        ms = jnp.mean(x * x, axis=-1, keepdims=True)
        h_ref[r:r + BF16_ROWS, :] = (x * lax.rsqrt(ms + EPS) * gain + shift).astype(BF16)


def _adaln_kernel(c_ref, w_ref, b_ref, o_ref):
    s = jax.nn.silu(c_ref[...])
    s1, s2, s3 = _split3(s)
    w = w_ref[...].astype(BF16)
    o_ref[...] = _dot(s1, w) + _dot(s2, w) + _dot(s3, w) + b_ref[...]


def _adaln(c8, w_mod, b_mod):
    d, n = w_mod.shape
    tn = 1024
    return pl.pallas_call(
        _adaln_kernel,
        grid=(n // tn,),
        in_specs=[pl.BlockSpec((8, d), lambda j: (0, 0)),
                  pl.BlockSpec((d, tn), lambda j: (0, j)),
                  pl.BlockSpec((1, tn), lambda j: (0, j))],
        out_specs=pl.BlockSpec((8, tn), lambda j: (0, j)),
        out_shape=jax.ShapeDtypeStruct((8, n), F32),
        compiler_params=_cparams(("arbitrary",)),
        name="adaln",
    )(c8, w_mod, b_mod.reshape(1, n))


def _ffn_kernel(x_ref, sh_ref, sc_ref, gt_ref, nw_ref, wg_ref, wu_ref, wd_ref, fw_ref, o_ref,
                h_scr, acc_scr, *, final_norm):
    j = pl.program_id(1)

    @pl.when(j == 0)
    def _():
        _rms_mod_store(x_ref, h_scr, nw_ref[...], sc_ref[...], sh_ref[...])
        acc_scr[...] = jnp.zeros_like(acc_scr)

    acts = []
    for s in range(wg_ref.shape[1] // SUB):
        cols = slice(s * SUB, (s + 1) * SUB)
        g = _dot(h_scr[...], wg_ref[:, cols])
        u = _dot(h_scr[...], wu_ref[:, cols])
        acts.append((jax.nn.silu(g) * u).astype(BF16))
    acc_scr[...] += _dot(jnp.concatenate(acts, axis=1), wd_ref[...])

    @pl.when(j == pl.num_programs(1) - 1)
    def _():
        half_gate = 0.5 * gt_ref[...]
        for r in range(0, x_ref.shape[0], SUBLANES):
            rows = slice(r, r + SUBLANES)
            out = x_ref[rows, :] + half_gate * acc_scr[rows, :]
            if final_norm:
                ms = jnp.mean(out * out, axis=-1, keepdims=True)
                out = out * lax.rsqrt(ms + EPS) * fw_ref[...]
            o_ref[rows, :] = out


def _ffn(x2, mod3, mod_k, mod_row, norm_w, wg, wu, wd, final_w, *, tm, tf):
    t, d = x2.shape
    f = wg.shape[1]
    final_norm = final_w is not None
    fw = final_w if final_norm else norm_w

    def mod_spec(k):
        return pl.BlockSpec((None, 1, d), lambda i, j: (mod_row(i), 0, k))

    return pl.pallas_call(
        functools.partial(_ffn_kernel, final_norm=final_norm),
        grid=(t // tm, f // tf),
        in_specs=[pl.BlockSpec((tm, d), lambda i, j: (i, 0)),
                  mod_spec(mod_k), mod_spec(mod_k + 1), mod_spec(mod_k + 2),
                  pl.BlockSpec((1, d), lambda i, j: (0, 0)),
                  pl.BlockSpec((d, tf), lambda i, j: (0, j)),
                  pl.BlockSpec((d, tf), lambda i, j: (0, j)),
                  pl.BlockSpec((tf, d), lambda i, j: (j, 0)),
                  pl.BlockSpec((1, d), lambda i, j: (0, 0))],
        out_specs=pl.BlockSpec((tm, d), lambda i, j: (i, 0)),
        out_shape=jax.ShapeDtypeStruct((t, d), F32),
        scratch_shapes=[pltpu.VMEM((tm, d), BF16), pltpu.VMEM((tm, d), F32)],
        compiler_params=_cparams(("parallel", "arbitrary")),
        name="ffn",
    )(x2, mod3, mod3, mod3, norm_w.reshape(1, d), wg, wu, wd, fw.reshape(1, d))


def _shift_in_segment(x4, off):
    nseg, _, sub, c = x4.shape
    r = pltpu.roll(x4, (-off) % sub, axis=2)
    zero = jnp.zeros((nseg, 1, sub, c), x4.dtype)
    row = lax.broadcasted_iota(jnp.int32, (1, 1, sub, c), 2)
    if off < 0:
        return jnp.where(row < -off, jnp.concatenate([zero, r[:, :-1]], axis=1), r)
    return jnp.where(row >= sub - off, jnp.concatenate([r[:, 1:], zero], axis=1), r)


def _inproj_kernel(x_ref, sh_ref, sc_ref, nw_ref, w_ref, wdt_ref, dtb_ref, cw_ref, cb_ref, bg_ref,
                   p_ref, dt_ref, h_scr, acc_scr, *, seg, j_gelu, j_plain, j_conv):
    j = pl.program_id(1)
    tm, tn = p_ref.shape

    @pl.when(j == 0)
    def _():
        _rms_mod_store(x_ref, h_scr, nw_ref[...], sc_ref[...], sh_ref[...])
        dt_ref[...] = jax.nn.softplus(_dot(h_scr[...], wdt_ref[...]) + dtb_ref[...])

    def run(epilogue):
        cols = [slice(s * PROJ_SUB, (s + 1) * PROJ_SUB) for s in range(tn // PROJ_SUB)]
        for s, c in enumerate(cols):
            if s > 0:
                p_ref[:, cols[s - 1]] = epilogue(acc_scr[(s - 1) % 2], cols[s - 1]).astype(BF16)
            acc_scr[s % 2] = _dot(h_scr[...], w_ref[:, c])
        last = len(cols) - 1
        p_ref[:, cols[last]] = epilogue(acc_scr[last % 2], cols[last]).astype(BF16)

    @pl.when(j < j_gelu)
    def _():
        run(lambda acc, cols: jax.nn.gelu(acc))

    @pl.when((j >= j_gelu) & (j < j_plain))
    def _():
        run(lambda acc, cols: acc)

    @pl.when((j >= j_plain) & (j < j_conv))
    def _():
        pad = (CONV_K - 1) // 2

        def conv_silu(acc, cols):
            x4 = acc.reshape(tm // seg, seg // SUBLANES, SUBLANES, PROJ_SUB)
            y = cb_ref[:, cols] + cw_ref[pad:pad + 1, cols] * x4
            for tap in range(CONV_K):
                if tap != pad:
                    y = y + cw_ref[tap:tap + 1, cols] * _shift_in_segment(x4, tap - pad)
            return jax.nn.silu(y).reshape(tm, PROJ_SUB)

        run(conv_silu)

    @pl.when(j >= j_conv)
    def _():
        run(lambda acc, cols: jax.nn.sigmoid(acc + bg_ref[:, cols]))


def _inproj(x2, mod3, mod_row, norm_w, w_main, w_dt, dt_bias, conv_w, conv_b, b_gate, *, seg, tm, tn,
            n_gelu, n_plain, n_conv):
    t, d = x2.shape
    n = w_main.shape[1]
    j_gelu = n_gelu // tn
    j_plain = j_gelu + n_plain // tn
    j_conv = j_plain + n_conv // tn
    nconv_t = n_conv // tn
    ngate_t = b_gate.shape[1] // tn
    assert t % tm == 0 and tm % seg == 0 and n % tn == 0

    def mod_spec(k):
        return pl.BlockSpec((None, 1, d), lambda i, j: (mod_row(i), 0, k))

    return pl.pallas_call(
        functools.partial(_inproj_kernel, seg=seg, j_gelu=j_gelu, j_plain=j_plain, j_conv=j_conv),
        grid=(t // tm, n // tn),
        in_specs=[pl.BlockSpec((tm, d), lambda i, j: (i, 0)),
                  mod_spec(3), mod_spec(4),
                  pl.BlockSpec((1, d), lambda i, j: (0, 0)),
                  pl.BlockSpec((d, tn), lambda i, j: (0, j)),
                  pl.BlockSpec((d, LANES), lambda i, j: (0, 0)),
                  pl.BlockSpec((1, LANES), lambda i, j: (0, 0)),
                  pl.BlockSpec((CONV_K, tn), lambda i, j: (0, jnp.clip(j - j_plain, 0, nconv_t - 1))),
                  pl.BlockSpec((1, tn), lambda i, j: (0, jnp.clip(j - j_plain, 0, nconv_t - 1))),
                  pl.BlockSpec((1, tn), lambda i, j: (0, jnp.clip(j - j_conv, 0, ngate_t - 1)))],
        out_specs=[pl.BlockSpec((tm, tn), lambda i, j: (i, j)),
                   pl.BlockSpec((tm, LANES), lambda i, j: (i, 0))],
        out_shape=[jax.ShapeDtypeStruct((t, n), BF16), jax.ShapeDtypeStruct((t, LANES), F32)],
        scratch_shapes=[pltpu.VMEM((tm, d), BF16), pltpu.VMEM((2, tm, PROJ_SUB), F32)],
        compiler_params=_cparams(("parallel", "arbitrary")),
        name="inproj",
    )(x2, mod3, mod3, norm_w.reshape(1, d), w_main, w_dt, dt_bias, conv_w, conv_b, b_gate)


def _gmlp_kernel(u_ref, v_ref, lw_ref, lb_ref, ws_ref, bs_ref, o_ref):
    v = v_ref[...].astype(F32)
    mu = jnp.mean(v, axis=-1, keepdims=True)
    vc = v - mu
    var = jnp.mean(vc * vc, axis=-1, keepdims=True)
    vn = (vc * lax.rsqrt(var + EPS) * lw_ref[...] + lb_ref[...]).astype(BF16)
    tm, width = vn.shape
    gw = width // A_GROUPS
    for c in range(tm // A_CHUNK):
        r0 = c * A_CHUNK
        for g in range(A_GROUPS):
            c0 = g * gw
            s = _dot(ws_ref[g], vn[r0:r0 + A_CHUNK, c0:c0 + gw]) + bs_ref[:, g:g + 1]
            u = u_ref[r0:r0 + A_CHUNK, c0:c0 + gw].astype(F32)
            o_ref[r0:r0 + A_CHUNK, c0:c0 + gw] = (u * s).astype(BF16)


def _gmlp(p, width, ln_w, ln_b, ws, bs_t, *, tm):
    t = p.shape[0]
    return pl.pallas_call(
        _gmlp_kernel,
        grid=(t // tm,),
        in_specs=[pl.BlockSpec((tm, width), lambda i: (i, 0)),
                  pl.BlockSpec((tm, width), lambda i: (i, 1)),
                  pl.BlockSpec((1, width), lambda i: (0, 0)),
                  pl.BlockSpec((1, width), lambda i: (0, 0)),
                  pl.BlockSpec((A_GROUPS, A_CHUNK, A_CHUNK), lambda i: (0, 0, 0)),
                  pl.BlockSpec((A_CHUNK, A_GROUPS), lambda i: (0, 0))],
        out_specs=pl.BlockSpec((tm, width), lambda i: (i, 0)),
        out_shape=jax.ShapeDtypeStruct((t, width), BF16),
        compiler_params=_cparams(("parallel",)),
        name="gmlp",
    )(p, p, ln_w.reshape(1, width), ln_b.reshape(1, width), ws, bs_t)


HPG = 8
GL = 2 * HPG


def _decay_tables(dt, alog_row, cs_scr, ecs_scr, rowt_scr):
    q = dt.shape[0]
    a = dt * (-jnp.exp(alog_row) * LOG2E)
    ri = lax.broadcasted_iota(jnp.int32, (q, q), 0)
    ci = lax.broadcasted_iota(jnp.int32, (q, q), 1)
    tril = jnp.where(ri >= ci, 1.0, 0.0).astype(BF16)
    triu = jnp.where(ri <= ci, 1.0, 0.0).astype(BF16)
    lane_f = (lax.broadcasted_iota(jnp.int32, (q, LANES), 1) % GL) < HPG
    sub_f = (lax.broadcasted_iota(jnp.int32, (LANES, q), 0) % GL) < HPG
    cs = jnp.where(lane_f, _dot_exact_lhs(tril, a), _dot_exact_lhs(triu, a))
    a_t = a.T
    cst = jnp.where(sub_f, _dot_exact_rhs(a_t, triu), _dot_exact_rhs(a_t, tril))
    rowt = cst - jnp.log2(dt.T)
    ecs = jnp.exp2(cs)
    for g in range(GROUPS):
        cs_scr[g] = cs[:, g * GL:(g + 1) * GL]
        ecs_scr[g] = ecs[:, g * GL:(g + 1) * GL]
        rowt_scr[g] = rowt[g * GL:(g + 1) * GL, :]


def _pair_masks(q):
    lane = lax.broadcasted_iota(jnp.int32, (q, LANES), 1)
    return (jnp.where(lane < HEADDIM, 1.0, 0.0).astype(BF16),
            jnp.where(lane >= HEADDIM, 1.0, 0.0).astype(BF16))


def _block_diag_pair(xp, masks):
    return jnp.concatenate([xp * masks[0], xp * masks[1]], axis=0)


def _head_terms(cs_g, rowt_g, hl, last):
    q = cs_g.shape[0]
    col = cs_g[:, hl:hl + 1]
    rowt = rowt_g[hl:hl + 1, :]
    clast = jnp.broadcast_to(col[last:last + 1, :], (1, q))
    return col, rowt, clast


def _ctx_state_kernel(x_ref, b_ref, dt_ref, alog_ref, h_ref, cs_scr, ecs_scr, rowt_scr):
    q = x_ref.shape[0]
    _decay_tables(dt_ref[...], alog_ref[...], cs_scr, ecs_scr, rowt_scr)
    masks = _pair_masks(q)

    def group_body(g, carry):
        cs_g, rowt_g = cs_scr[g], rowt_scr[g]
        bg = b_ref[:, pl.ds(pl.multiple_of(g * STATE, STATE), STATE)]
        bg_t = bg.astype(F32).T.astype(BF16)
        for d in range(2):
            last = q - 1 if d == 0 else 0
            for pair in range(HPG // 2):
                col0 = pl.multiple_of(g * (HPG * HEADDIM) + pair * LANES, LANES)
                bd = _block_diag_pair(x_ref[:, pl.ds(col0, LANES)], masks)
                bts = []
                for k in range(2):
                    _, rowt, clast = _head_terms(cs_g, rowt_g, d * HPG + pair * 2 + k, last)
                    bts.append(bg_t * jnp.exp2(clast - rowt).astype(BF16))
                h_ref[d, g, :, pair * LANES:(pair + 1) * LANES] = _dot(jnp.concatenate(bts, axis=1), bd)
        return carry

    lax.fori_loop(0, GROUPS, group_body, 0)


def _ctx_states(pc, dtc, alog_row, *, bsz, q, x_blk, b_blk):
    inner = GROUPS * HPG * HEADDIM
    return pl.pallas_call(
        _ctx_state_kernel,
        grid=(bsz,),
        in_specs=[pl.BlockSpec((q, inner), lambda b: (b, x_blk)),
                  pl.BlockSpec((q, GROUPS * STATE), lambda b: (b, b_blk)),
                  pl.BlockSpec((q, LANES), lambda b: (b, 0)),
                  pl.BlockSpec((1, LANES), lambda b: (0, 0))],
        out_specs=pl.BlockSpec((2, None, GROUPS, STATE, HPG * HEADDIM), lambda b: (0, b, 0, 0, 0)),
        out_shape=jax.ShapeDtypeStruct((2, bsz, GROUPS, STATE, HPG * HEADDIM), F32),
        scratch_shapes=[pltpu.VMEM((GROUPS, q, GL), F32), pltpu.VMEM((GROUPS, q, GL), F32),
                        pltpu.VMEM((GROUPS, GL, q), F32)],
        compiler_params=_cparams(("arbitrary",)),
        name="ctx_state",
    )(pc, pc, dtc, alog_row)


def _ssd_kernel(xf_ref, bf_ref, cf_ref, dtf_ref, xb_ref, bb_ref, cb_ref, dtb_ref, alog_ref, dskip_ref, h0_ref,
                yf_ref, yb_ref, s_scr, cs_scr, ecs_scr, rowt_scr):
    q = xf_ref.shape[0]

    @pl.when(pl.program_id(1) == 0)
    def _():
        s_scr[...] = h0_ref[...]

    lane_f = (lax.broadcasted_iota(jnp.int32, (q, LANES), 1) % GL) < HPG
    dt = jnp.where(lane_f, dtf_ref[...], dtb_ref[...])
    _decay_tables(dt, alog_ref[...], cs_scr, ecs_scr, rowt_scr)

    ri = lax.broadcasted_iota(jnp.int32, (q, q), 0)
    ci = lax.broadcasted_iota(jnp.int32, (q, q), 1)
    lane_lo = lax.broadcasted_iota(jnp.int32, (q, LANES), 1) < HEADDIM
    lane_lo_row = lax.broadcasted_iota(jnp.int32, (1, LANES), 1) < HEADDIM
    masks = _pair_masks(q)
    dirs = ((xf_ref, bf_ref, cf_ref, yf_ref), (xb_ref, bb_ref, cb_ref, yb_ref))

    def group_body(g, carry):
        cs_g, ecs_g, rowt_g = cs_scr[g], ecs_scr[g], rowt_scr[g]
        for d, (x_ref, b_ref, c_ref, y_ref) in enumerate(dirs):
            last = q - 1 if d == 0 else 0
            mask = (ri >= ci) if d == 0 else (ri <= ci)
            gcol = slice(g * STATE, (g + 1) * STATE)
            bg = b_ref[:, gcol]
            cg = c_ref[:, gcol]
            cb = lax.dot_general(cg, bg, (((1,), (1,)), ((), ())), preferred_element_type=F32)
            bg_t = bg.astype(F32).T.astype(BF16)
            for pair in range(HPG // 2):
                psl = slice(pair * LANES, (pair + 1) * LANES)
                s_pair = s_scr[d, g, :, psl]
                y_off = _dot(cg, s_pair.astype(BF16))
                col0 = g * (HPG * HEADDIM) + pair * LANES
                xp = x_ref[:, pl.ds(col0, LANES)]
                bd = _block_diag_pair(xp, masks)
                ms, bts, ecols, elasts = [], [], [], []
                for k in range(2):
                    hl = d * HPG + pair * 2 + k
                    col, rowt, clast = _head_terms(cs_g, rowt_g, hl, last)
                    colb = jnp.broadcast_to(col, (q, q))
                    ms.append((cb * jnp.exp2(jnp.where(mask, colb - rowt, -jnp.inf))).astype(BF16))
                    bts.append(bg_t * jnp.exp2(clast - rowt).astype(BF16))
                    ecols.append(jnp.exp2(colb))
                    elasts.append(jnp.exp2(clast))
                y = _dot(jnp.concatenate(ms, axis=1), bd)
                y = y + y_off * jnp.where(lane_lo, ecols[0], ecols[1])
                if d == 0:
                    y = y + xp.astype(F32) * dskip_ref[:, pl.ds(col0, LANES)]
                y_ref[:, pl.ds(col0, LANES)] = y.astype(BF16)
                grow = jnp.where(lane_lo_row, elasts[0], elasts[1])
                s_scr[d, g, :, psl] = s_pair * grow + _dot(jnp.concatenate(bts, axis=1), bd)
        return carry

    for g in range(GROUPS):
        group_body(g, 0)


def _ssd(p, dt, alog_row, dskip_row, h0, *, bsz, nc, x_blk, b_blk, c_blk):
    q = SSD_CHUNK
    t = p.shape[0]
    inner = GROUPS * HPG * HEADDIM
    gs = GROUPS * STATE

    def fwd(b, c):
        return b * nc + c

    def bwd(b, c):
        return b * nc + (nc - 1 - c)

    def specs(row):
        return [pl.BlockSpec((q, inner), lambda b, c: (row(b, c), x_blk)),
                pl.BlockSpec((q, gs), lambda b, c: (row(b, c), b_blk)),
                pl.BlockSpec((q, gs), lambda b, c: (row(b, c), c_blk)),
                pl.BlockSpec((q, LANES), lambda b, c: (row(b, c), 0))]

    return pl.pallas_call(
        _ssd_kernel,
        grid=(bsz, nc),
        in_specs=specs(fwd) + specs(bwd) + [
            pl.BlockSpec((1, LANES), lambda b, c: (0, 0)),
            pl.BlockSpec((1, inner), lambda b, c: (0, 0)),
            pl.BlockSpec((2, None, GROUPS, STATE, HPG * HEADDIM), lambda b, c: (0, b, 0, 0, 0))],
        out_specs=[pl.BlockSpec((q, inner), lambda b, c: (fwd(b, c), 0)),
                   pl.BlockSpec((q, inner), lambda b, c: (bwd(b, c), 0))],
        out_shape=[jax.ShapeDtypeStruct((t, inner), BF16), jax.ShapeDtypeStruct((t, inner), BF16)],
        scratch_shapes=[pltpu.VMEM((2, GROUPS, STATE, HPG * HEADDIM), F32),
                        pltpu.VMEM((GROUPS, q, GL), F32), pltpu.VMEM((GROUPS, q, GL), F32),
                        pltpu.VMEM((GROUPS, GL, q), F32)],
        compiler_params=_cparams(("arbitrary", "arbitrary")),
        name="ssd",
    )(p, p, p, dt, p, p, p, dt, alog_row, dskip_row, h0)


def _gnorm_kernel(yf_ref, yb_ref, z_ref, w_ref, o_ref):
    y = (yf_ref[...].astype(F32) + yb_ref[...].astype(F32)) * jax.nn.silu(z_ref[...].astype(F32))
    gw = y.shape[1] // GROUPS
    for g in range(GROUPS):
        yg = y[:, g * gw:(g + 1) * gw]
        ms = jnp.mean(yg * yg, axis=-1, keepdims=True)
        o_ref[:, g * gw:(g + 1) * gw] = (yg * lax.rsqrt(ms + EPS) * w_ref[:, g * gw:(g + 1) * gw]).astype(BF16)


def _gnorm(yf, yb, p, z_blk, norm_w, *, tm):
    t, inner = yf.shape
    return pl.pallas_call(
        _gnorm_kernel,
        grid=(t // tm,),
        in_specs=[pl.BlockSpec((tm, inner), lambda i: (i, 0)),
                  pl.BlockSpec((tm, inner), lambda i: (i, 0)),
                  pl.BlockSpec((tm, inner), lambda i: (i, z_blk)),
                  pl.BlockSpec((1, inner), lambda i: (0, 0))],
        out_specs=pl.BlockSpec((tm, inner), lambda i: (i, 0)),
        out_shape=jax.ShapeDtypeStruct((t, inner), BF16),
        compiler_params=_cparams(("parallel",)),
        name="gnorm",
    )(yf, yb, p, norm_w.reshape(1, inner))


def _merge_kernel(us_ref, yn_ref, ga_ref, gb_ref, wa_ref, wb_ref, o_ref):
    for s in range(o_ref.shape[1] // SUB):
        cols = slice(s * SUB, (s + 1) * SUB)
        ya = _dot(us_ref[...], wa_ref[:, cols])
        yb = _dot(yn_ref[...], wb_ref[:, cols])
        o_ref[:, cols] = (ga_ref[:, cols].astype(F32) * ya + gb_ref[:, cols].astype(F32) * yb).astype(BF16)


def _merge(us, yn, p, gate_col0, wa, wb, *, tm, tn):
    t = us.shape[0]
    wa_in, d = wa.shape
    wb_in = wb.shape[0]
    ga_blk = gate_col0 // tn
    gb_blk = (gate_col0 + d) // tn
    return pl.pallas_call(
        _merge_kernel,
        grid=(t // tm, d // tn),
        in_specs=[pl.BlockSpec((tm, wa_in), lambda i, j: (i, 0)),
                  pl.BlockSpec((tm, wb_in), lambda i, j: (i, 0)),
                  pl.BlockSpec((tm, tn), lambda i, j: (i, ga_blk + j)),
                  pl.BlockSpec((tm, tn), lambda i, j: (i, gb_blk + j)),
                  pl.BlockSpec((wa_in, tn), lambda i, j: (0, j)),
                  pl.BlockSpec((wb_in, tn), lambda i, j: (0, j))],
        out_specs=pl.BlockSpec((tm, tn), lambda i, j: (i, j)),
        out_shape=jax.ShapeDtypeStruct((t, d), BF16),
        compiler_params=_cparams(("parallel", "arbitrary")),
        name="merge",
    )(us, yn, p, p, wa, wb)


def _outproj_kernel(mg_ref, x_ref, m_ref, wo_ref, o_ref):
    for s in range(o_ref.shape[1] // OUT_SUB):
        cols = slice(s * OUT_SUB, (s + 1) * OUT_SUB)
        o_ref[:, cols] = x_ref[:, cols] + m_ref[:, cols] * _dot(mg_ref[...], wo_ref[:, cols])


def _outproj(mg, x2, mod3, mod_row, wo, *, tm):
    t, d = x2.shape
    return pl.pallas_call(
        _outproj_kernel,
        grid=(t // tm,),
        in_specs=[pl.BlockSpec((tm, d), lambda i: (i, 0)),
                  pl.BlockSpec((tm, d), lambda i: (i, 0)),
                  pl.BlockSpec((None, 1, d), lambda i: (mod_row(i), 0, 5)),
                  pl.BlockSpec((d, d), lambda i: (0, 0))],
        out_specs=pl.BlockSpec((tm, d), lambda i: (i, 0)),
        out_shape=jax.ShapeDtypeStruct((t, d), F32),
        compiler_params=_cparams(("parallel",)),
        name="outproj",
    )(mg, x2, mod3, wo)


def _dt_perm(a):
    lead = a.shape[:-2]
    a = a.reshape(*lead, 2, GROUPS, HPG)
    a = jnp.swapaxes(a, -3, -2)
    return a.reshape(*lead, 2 * GROUPS * HPG)


def kernel(x, c, ctx, c_ctx, w_mod, b_mod, norm_ffn1, ffn1_gate, ffn1_up, ffn1_down, norm_mix, w_in, b_gate,
           gmlp_ln_w, gmlp_ln_b, gmlp_ws, gmlp_bs, w_a, conv_w, conv_b, a_log, dt_bias, d_skip, ssm_norm,
           w_b, w_out, norm_ffn2, ffn2_gate, ffn2_up, ffn2_down, norm_final):
    bsz, seq, d = x.shape
    ctx_len = ctx.shape[1]
    depth = w_mod.shape[0]
    assert depth == 1, "context stream update between layers is not implemented"
    assert bsz < 8
    inner = GROUPS * HPG * HEADDIM
    gs = GROUPS * STATE
    a_width = w_a.shape[1]
    n_heads = GROUPS * HPG
    assert w_b.shape[1] == inner and conv_w.shape[2] == inner + 2 * gs
    assert w_in.shape[2] == 2 * a_width + 2 * inner + 2 * gs + 2 * n_heads + 2 * d

    t = bsz * seq
    tc = bsz * ctx_len
    x2 = x.reshape(t, d)
    ctx2 = ctx.reshape(tc, d)

    off_dt = 2 * a_width + 2 * inner + 2 * gs
    off_gate = off_dt + 2 * n_heads
    n_gelu, n_plain, n_conv = 2 * a_width, inner, inner + 2 * gs
    col_z = n_gelu
    col_x = n_gelu + n_plain
    col_b = col_x + inner
    col_c = col_b + gs
    col_gate = n_gelu + n_plain + n_conv

    i = 0
    w_main = jnp.concatenate([w_in[i][:, :off_dt], w_in[i][:, off_gate:]], axis=1).astype(BF16)
    w_dt = _dt_perm(w_in[i][:, off_dt:off_gate].reshape(d, 2, n_heads)).astype(BF16)
    dt_bias_row = _dt_perm(dt_bias[i]).reshape(1, 2 * n_heads)
    alog_row = _dt_perm(a_log[i].astype(F32)).reshape(1, 2 * n_heads)
    dskip_row = jnp.repeat(d_skip[i], HEADDIM).reshape(1, inner)
    bf = lambda w: w.astype(BF16)

    c8 = jnp.zeros((8, d), F32).at[:bsz].set(c).at[bsz].set(c_ctx)
    mod3 = _adaln(c8, w_mod[i], b_mod[i]).reshape(8, 1, N_MOD * d)

    tm = 512
    tm_proj = 512

    def lat_row(rows):
        assert seq % rows == 0, "a token tile must not straddle two samples"
        return lambda ti: (ti * rows) // seq

    ctx_row = lambda ti: bsz

    ffn1_w = (bf(ffn1_gate[i]), bf(ffn1_up[i]), bf(ffn1_down[i]))
    x1 = _ffn(x2, mod3, 0, lat_row(tm), norm_ffn1[i], *ffn1_w, None, tm=tm, tf=512)
    ctx1 = _ffn(ctx2, mod3, 0, ctx_row, norm_ffn1[i], *ffn1_w, None, tm=tm, tf=512)

    proj_args = (norm_mix[i], w_main, w_dt, dt_bias_row, conv_w[i], conv_b[i].reshape(1, -1), b_gate[i].reshape(1, -1))
    proj_kw = dict(tm=tm_proj, tn=2048, n_gelu=n_gelu, n_plain=n_plain, n_conv=n_conv)
    pc, dtc = _inproj(ctx1, mod3, ctx_row, *proj_args, seg=ctx_len, **proj_kw)
    h0 = _ctx_states(pc, dtc, alog_row, bsz=bsz, q=ctx_len, x_blk=col_x // inner, b_blk=col_b // gs)

    p, dt = _inproj(x1, mod3, lat_row(tm_proj), *proj_args, seg=GRID_W, **proj_kw)
    us = _gmlp(p, a_width, gmlp_ln_w[i], gmlp_ln_b[i], bf(gmlp_ws[i]), gmlp_bs[i].T, tm=tm)
    yf, yb = _ssd(p, dt, alog_row, dskip_row, h0, bsz=bsz, nc=seq // SSD_CHUNK,
                  x_blk=col_x // inner, b_blk=col_b // gs, c_blk=col_c // gs)
    yn = _gnorm(yf, yb, p, col_z // inner, ssm_norm[i], tm=256)
    mg = _merge(us, yn, p, col_gate, bf(w_a[i]), bf(w_b[i]), tm=1024, tn=512)
    x3 = _outproj(mg, x1, mod3, lat_row(tm), bf(w_out[i]), tm=tm)

    out = _ffn(x3, mod3, 6, lat_row(tm), norm_ffn2[i], bf(ffn2_gate[i]), bf(ffn2_up[i]), bf(ffn2_down[i]),
               norm_final, tm=tm, tf=512)
    return out.reshape(bsz, seq, d)
```

```python
import functools

import jax
import jax.numpy as jnp
from jax import lax
from jax.experimental import pallas as pl
from jax.experimental.pallas import tpu as pltpu

F32 = jnp.float32
BF16 = jnp.bfloat16
EPS = 1e-6
LOG2E = 1.4426950408889634

GRID_W = 64
A_GROUPS = 8
A_CHUNK = 128
HEADDIM = 64
GROUPS = 8
STATE = 128
CONV_K = 5
SSD_CHUNK = 128
N_MOD = 9

LANES = 128
SUB = 256
OUT_SUB = 512
PROJ_SUB = 512
PROJ_CHUNKS = (512, 512, 512, 256, 256)
SUBLANES = 8
BF16_ROWS = 16
VMEM_LIMIT = 56 * 1024 * 1024


def _cparams(sem):
    return pltpu.CompilerParams(dimension_semantics=sem, vmem_limit_bytes=VMEM_LIMIT)


def _dot(a, b):
    return jnp.dot(a, b, preferred_element_type=F32)


def _split3(a):
    a1 = a.astype(BF16)
    r1 = a - a1.astype(F32)
    a2 = r1.astype(BF16)
    r2 = r1 - a2.astype(F32)
    return a1, a2, r2.astype(BF16)


def _dot_exact_lhs(tri, a):
    a1, a2, a3 = _split3(a)
    return _dot(tri, a1) + _dot(tri, a2) + _dot(tri, a3)


def _dot_exact_rhs(a, tri):
    a1, a2, a3 = _split3(a)
    return _dot(a1, tri) + _dot(a2, tri) + _dot(a3, tri)


def _rms_mod_store(x_ref, h_ref, nw, scale, shift):
    gain = nw * (1.0 + scale)
    for r in range(0, x_ref.shape[0], BF16_ROWS):
        x = x_ref[r:r + BF16_ROWS, :]
        ms = jnp.mean(x * x, axis=-1, keepdims=True)
        h_ref[r:r + BF16_ROWS, :] = (x * lax.rsqrt(ms + EPS) * gain + shift).astype(BF16)


def _adaln_kernel(c_ref, w_ref, b_ref, o_ref):
    s = jax.nn.silu(c_ref[...])
    s1, s2, s3 = _split3(s)
    w = w_ref[...].astype(BF16)
    o_ref[...] = _dot(s1, w) + _dot(s2, w) + _dot(s3, w) + b_ref[...]


def _adaln(c8, w_mod, b_mod):
    d, n = w_mod.shape
    tn = 1024
    return pl.pallas_call(
        _adaln_kernel,
        grid=(n // tn,),
        in_specs=[pl.BlockSpec((8, d), lambda j: (0, 0)),
                  pl.BlockSpec((d, tn), lambda j: (0, j)),
                  pl.BlockSpec((1, tn), lambda j: (0, j))],
        out_specs=pl.BlockSpec((8, tn), lambda j: (0, j)),
        out_shape=jax.ShapeDtypeStruct((8, n), F32),
        compiler_params=_cparams(("arbitrary",)),
        name="adaln",
    )(c8, w_mod, b_mod.reshape(1, n))


def _ffn_kernel(x_ref, sh_ref, sc_ref, gt_ref, nw_ref, wg_ref, wu_ref, wd_ref, fw_ref, o_ref,
                h_scr, acc_scr, *, final_norm):
    j = pl.program_id(1)

    @pl.when(j == 0)
    def _():
        _rms_mod_store(x_ref, h_scr, nw_ref[...], sc_ref[...], sh_ref[...])
        acc_scr[...] = jnp.zeros_like(acc_scr)

    acts = []
    for s in range(wg_ref.shape[1] // SUB):
        cols = slice(s * SUB, (s + 1) * SUB)
        g = _dot(h_scr[...], wg_ref[:, cols])
        u = _dot(h_scr[...], wu_ref[:, cols])
        acts.append((jax.nn.silu(g) * u).astype(BF16))
    acc_scr[...] += _dot(jnp.concatenate(acts, axis=1), wd_ref[...])

    @pl.when(j == pl.num_programs(1) - 1)
    def _():
        half_gate = 0.5 * gt_ref[...]
        for r in range(0, x_ref.shape[0], SUBLANES):
            rows = slice(r, r + SUBLANES)
            out = x_ref[rows, :] + half_gate * acc_scr[rows, :]
            if final_norm:
                ms = jnp.mean(out * out, axis=-1, keepdims=True)
                out = out * lax.rsqrt(ms + EPS) * fw_ref[...]
            o_ref[rows, :] = out


def _ffn(x2, mod3, mod_k, mod_row, norm_w, wg, wu, wd, final_w, *, tm, tf):
    t, d = x2.shape
    f = wg.shape[1]
    final_norm = final_w is not None
    fw = final_w if final_norm else norm_w

    def mod_spec(k):
        return pl.BlockSpec((None, 1, d), lambda i, j: (mod_row(i), 0, k))

    return pl.pallas_call(
        functools.partial(_ffn_kernel, final_norm=final_norm),
        grid=(t // tm, f // tf),
        in_specs=[pl.BlockSpec((tm, d), lambda i, j: (i, 0)),
                  mod_spec(mod_k), mod_spec(mod_k + 1), mod_spec(mod_k + 2),
                  pl.BlockSpec((1, d), lambda i, j: (0, 0)),
                  pl.BlockSpec((d, tf), lambda i, j: (0, j)),
                  pl.BlockSpec((d, tf), lambda i, j: (0, j)),
                  pl.BlockSpec((tf, d), lambda i, j: (j, 0)),
                  pl.BlockSpec((1, d), lambda i, j: (0, 0))],
        out_specs=pl.BlockSpec((tm, d), lambda i, j: (i, 0)),
        out_shape=jax.ShapeDtypeStruct((t, d), F32),
        scratch_shapes=[pltpu.VMEM((tm, d), BF16), pltpu.VMEM((tm, d), F32)],
        compiler_params=_cparams(("parallel", "arbitrary")),
        name="ffn",
    )(x2, mod3, mod3, mod3, norm_w.reshape(1, d), wg, wu, wd, fw.reshape(1, d))


def _shift_in_segment(x4, off):
    nseg, _, sub, c = x4.shape
    r = pltpu.roll(x4, (-off) % sub, axis=2)
    zero = jnp.zeros((nseg, 1, sub, c), x4.dtype)
    row = lax.broadcasted_iota(jnp.int32, (1, 1, sub, c), 2)
    if off < 0:
        return jnp.where(row < -off, jnp.concatenate([zero, r[:, :-1]], axis=1), r)
    return jnp.where(row >= sub - off, jnp.concatenate([r[:, 1:], zero], axis=1), r)


def _inproj_kernel(x_ref, sh_ref, sc_ref, nw_ref, w_ref, wdt_ref, dtb_ref, cw_ref, cb_ref,
                   p_ref, dt_ref, h_scr, acc_scr, *, seg, j0, j_gelu, j_plain, j_conv):
    j = pl.program_id(1) + j0
    tm, tn = p_ref.shape

    @pl.when(j == j0)
    def _():
        _rms_mod_store(x_ref, h_scr, nw_ref[...], sc_ref[...], sh_ref[...])
        dt_ref[...] = jax.nn.softplus(_dot(h_scr[...], wdt_ref[...]) + dtb_ref[...])

    def run(epilogue):
        assert sum(PROJ_CHUNKS) == tn
        starts = [sum(PROJ_CHUNKS[:k]) for k in range(len(PROJ_CHUNKS))]
        cols = [slice(a, a + w) for a, w in zip(starts, PROJ_CHUNKS)]

        def finish(s):
            c = cols[s]
            p_ref[:, c] = epilogue(acc_scr[s % 2, :, :PROJ_CHUNKS[s]], c).astype(BF16)

        for s, c in enumerate(cols):
            if s > 0:
                finish(s - 1)
            acc_scr[s % 2, :, :PROJ_CHUNKS[s]] = _dot(h_scr[...], w_ref[:, c])
        finish(len(cols) - 1)

    @pl.when(j < j_gelu)
    def _():
        run(lambda acc, cols: jax.nn.gelu(acc))

    @pl.when(((j >= j_gelu) & (j < j_plain)) | (j >= j_conv))
    def _():
        run(lambda acc, cols: acc)

    @pl.when((j >= j_plain) & (j < j_conv))
    def _():
        pad = (CONV_K - 1) // 2

        def conv_silu(acc, cols):
            x4 = acc.reshape(tm // seg, seg // SUBLANES, SUBLANES, acc.shape[1])
            y = cb_ref[:, cols] + cw_ref[pad:pad + 1, cols] * x4
            for tap in range(CONV_K):
                if tap != pad:
                    y = y + cw_ref[tap:tap + 1, cols] * _shift_in_segment(x4, tap - pad)
            return jax.nn.silu(y).reshape(acc.shape)

        run(conv_silu)


def _inproj(x2, mod3, mod_row, norm_w, w_all, w_dt, dt_bias, conv_w, conv_b, *, seg, tm, tn,
            n_gelu, n_plain, n_conv, n_gate, tiles=None):
    t, d = x2.shape
    j_gelu = n_gelu // tn
    j_plain = j_gelu + n_plain // tn
    j_conv = j_plain + n_conv // tn
    nconv_t = n_conv // tn
    ngate_t = n_gate // tn
    n_dt = w_dt.shape[1]
    j0, nj = tiles if tiles is not None else (0, j_conv + ngate_t)
    assert t % tm == 0 and tm % seg == 0 and w_all.shape[1] == (j_conv + ngate_t) * tn + n_dt

    def mod_spec(k):
        return pl.BlockSpec((None, 1, d), lambda i, j: (mod_row(i), 0, k))

    def w_col(j):
        return pl.multiple_of(jnp.where(j < j_conv, j * tn, j * tn + n_dt), LANES)

    return pl.pallas_call(
        functools.partial(_inproj_kernel, seg=seg, j0=j0, j_gelu=j_gelu, j_plain=j_plain, j_conv=j_conv),
        grid=(t // tm, nj),
        in_specs=[pl.BlockSpec((tm, d), lambda i, j: (i, 0)),
                  mod_spec(3), mod_spec(4),
                  pl.BlockSpec((1, d), lambda i, j: (0, 0)),
                  pl.BlockSpec((pl.Element(d), pl.Element(tn)), lambda i, j: (0, w_col(j + j0))),
                  pl.BlockSpec((d, LANES), lambda i, j: (0, 0)),
                  pl.BlockSpec((1, LANES), lambda i, j: (0, 0)),
                  pl.BlockSpec((CONV_K, tn), lambda i, j: (0, jnp.clip(j + j0 - j_plain, 0, nconv_t - 1))),
                  pl.BlockSpec((1, tn), lambda i, j: (0, jnp.clip(j + j0 - j_plain, 0, nconv_t - 1)))],
        out_specs=[pl.BlockSpec((tm, tn), lambda i, j: (i, j)),
                   pl.BlockSpec((tm, LANES), lambda i, j: (i, 0))],
        out_shape=[jax.ShapeDtypeStruct((t, nj * tn), BF16), jax.ShapeDtypeStruct((t, LANES), F32)],
        scratch_shapes=[pltpu.VMEM((tm, d), BF16), pltpu.VMEM((2, tm, PROJ_SUB), F32)],
        compiler_params=_cparams(("parallel", "arbitrary")),
        name="inproj",
    )(x2, mod3, mod3, norm_w.reshape(1, d), w_all, w_dt, dt_bias, conv_w, conv_b)


def _gmlp_kernel(u_ref, v_ref, lw_ref, lb_ref, ws_ref, bs_ref, o_ref):
    v = v_ref[...].astype(F32)
    mu = jnp.mean(v, axis=-1, keepdims=True)
    vc = v - mu
    var = jnp.mean(vc * vc, axis=-1, keepdims=True)
    vn = (vc * lax.rsqrt(var + EPS) * lw_ref[...] + lb_ref[...]).astype(BF16)
    tm, width = vn.shape
    gw = width // A_GROUPS
    for c in range(tm // A_CHUNK):
        r0 = c * A_CHUNK
        for g in range(A_GROUPS):
            c0 = g * gw
            s = _dot(ws_ref[g], vn[r0:r0 + A_CHUNK, c0:c0 + gw]) + bs_ref[:, g:g + 1]
            u = u_ref[r0:r0 + A_CHUNK, c0:c0 + gw].astype(F32)
            o_ref[r0:r0 + A_CHUNK, c0:c0 + gw] = (u * s).astype(BF16)


def _gmlp(p, width, ln_w, ln_b, ws, bs_t, *, tm):
    t = p.shape[0]
    return pl.pallas_call(
        _gmlp_kernel,
        grid=(t // tm,),
        in_specs=[pl.BlockSpec((tm, width), lambda i: (i, 0)),
                  pl.BlockSpec((tm, width), lambda i: (i, 1)),
                  pl.BlockSpec((1, width), lambda i: (0, 0)),
                  pl.BlockSpec((1, width), lambda i: (0, 0)),
                  pl.BlockSpec((A_GROUPS, A_CHUNK, A_CHUNK), lambda i: (0, 0, 0)),
                  pl.BlockSpec((A_CHUNK, A_GROUPS), lambda i: (0, 0))],
        out_specs=pl.BlockSpec((tm, width), lambda i: (i, 0)),
        out_shape=jax.ShapeDtypeStruct((t, width), BF16),
        compiler_params=_cparams(("parallel",)),
        name="gmlp",
    )(p, p, ln_w.reshape(1, width), ln_b.reshape(1, width), ws, bs_t)


HPG = 8
GL = 2 * HPG


def _decay_tables(dt, alog_row, cs_scr, ecs_scr, rowt_scr):
    q = dt.shape[0]
    a = dt * (-jnp.exp(alog_row) * LOG2E)
    ri = lax.broadcasted_iota(jnp.int32, (q, q), 0)
    ci = lax.broadcasted_iota(jnp.int32, (q, q), 1)
    tril = jnp.where(ri >= ci, 1.0, 0.0).astype(BF16)
    triu = jnp.where(ri <= ci, 1.0, 0.0).astype(BF16)
    lane_f = (lax.broadcasted_iota(jnp.int32, (q, LANES), 1) % GL) < HPG
    sub_f = (lax.broadcasted_iota(jnp.int32, (LANES, q), 0) % GL) < HPG
    cs = jnp.where(lane_f, _dot_exact_lhs(tril, a), _dot_exact_lhs(triu, a))
    a_t = a.T
    cst = jnp.where(sub_f, _dot_exact_rhs(a_t, triu), _dot_exact_rhs(a_t, tril))
    rowt = cst - jnp.log2(dt.T)
    ecs = jnp.exp2(cs)
    for g in range(GROUPS):
        cs_scr[g] = cs[:, g * GL:(g + 1) * GL]
        ecs_scr[g] = ecs[:, g * GL:(g + 1) * GL]
        rowt_scr[g] = rowt[g * GL:(g + 1) * GL, :]


def _pair_masks(q):
    lane = lax.broadcasted_iota(jnp.int32, (q, LANES), 1)
    return (jnp.where(lane < HEADDIM, 1.0, 0.0).astype(BF16),
            jnp.where(lane >= HEADDIM, 1.0, 0.0).astype(BF16))


def _block_diag_pair(xp, masks):
    return jnp.concatenate([xp * masks[0], xp * masks[1]], axis=0)


def _head_terms(cs_g, rowt_g, hl, last):
    q = cs_g.shape[0]
    col = cs_g[:, hl:hl + 1]
    rowt = rowt_g[hl:hl + 1, :]
    clast = jnp.broadcast_to(col[last:last + 1, :], (1, q))
    return col, rowt, clast


def _ctx_state_kernel(x_ref, b_ref, dt_ref, alog_ref, h_ref, cs_scr, ecs_scr, rowt_scr):
    q = x_ref.shape[0]
    _decay_tables(dt_ref[...], alog_ref[...], cs_scr, ecs_scr, rowt_scr)
    masks = _pair_masks(q)

    def group_body(g, carry):
        cs_g, rowt_g = cs_scr[g], rowt_scr[g]
        bg = b_ref[:, pl.ds(pl.multiple_of(g * STATE, STATE), STATE)]
        bg_t = bg.astype(F32).T.astype(BF16)
        for d in range(2):
            last = q - 1 if d == 0 else 0
            for pair in range(HPG // 2):
                col0 = pl.multiple_of(g * (HPG * HEADDIM) + pair * LANES, LANES)
                bd = _block_diag_pair(x_ref[:, pl.ds(col0, LANES)], masks)
                bts = []
                for k in range(2):
                    _, rowt, clast = _head_terms(cs_g, rowt_g, d * HPG + pair * 2 + k, last)
                    bts.append(bg_t * jnp.exp2(clast - rowt).astype(BF16))
                h_ref[d, g, :, pair * LANES:(pair + 1) * LANES] = _dot(jnp.concatenate(bts, axis=1), bd)
        return carry

    lax.fori_loop(0, GROUPS, group_body, 0)


def _ctx_states(pc, dtc, alog_row, *, bsz, q, x_blk, b_blk):
    inner = GROUPS * HPG * HEADDIM
    return pl.pallas_call(
        _ctx_state_kernel,
        grid=(bsz,),
        in_specs=[pl.BlockSpec((q, inner), lambda b: (b, x_blk)),
                  pl.BlockSpec((q, GROUPS * STATE), lambda b: (b, b_blk)),
                  pl.BlockSpec((q, LANES), lambda b: (b, 0)),
                  pl.BlockSpec((1, LANES), lambda b: (0, 0))],
        out_specs=pl.BlockSpec((2, None, GROUPS, STATE, HPG * HEADDIM), lambda b: (0, b, 0, 0, 0)),
        out_shape=jax.ShapeDtypeStruct((2, bsz, GROUPS, STATE, HPG * HEADDIM), F32),
        scratch_shapes=[pltpu.VMEM((GROUPS, q, GL), F32), pltpu.VMEM((GROUPS, q, GL), F32),
                        pltpu.VMEM((GROUPS, GL, q), F32)],
        compiler_params=_cparams(("arbitrary",)),
        name="ctx_state",
    )(pc, pc, dtc, alog_row)


def _ssd_kernel(xf_ref, bf_ref, cf_ref, dtf_ref, xb_ref, bb_ref, cb_ref, dtb_ref, alog_ref, dskip_ref, h0_ref,
                yf_ref, yb_ref, s_scr, cs_scr, ecs_scr, rowt_scr):
    q = xf_ref.shape[0]

    @pl.when(pl.program_id(1) == 0)
    def _():
        s_scr[...] = h0_ref[...]

    lane_f = (lax.broadcasted_iota(jnp.int32, (q, LANES), 1) % GL) < HPG
    dt = jnp.where(lane_f, dtf_ref[...], dtb_ref[...])
    _decay_tables(dt, alog_ref[...], cs_scr, ecs_scr, rowt_scr)

    ri = lax.broadcasted_iota(jnp.int32, (q, q), 0)
    ci = lax.broadcasted_iota(jnp.int32, (q, q), 1)
    lane_lo = lax.broadcasted_iota(jnp.int32, (q, LANES), 1) < HEADDIM
    lane_lo_row = lax.broadcasted_iota(jnp.int32, (1, LANES), 1) < HEADDIM
    masks = _pair_masks(q)
    dirs = ((xf_ref, bf_ref, cf_ref, yf_ref), (xb_ref, bb_ref, cb_ref, yb_ref))

    def group_body(g, carry):
        cs_g, ecs_g, rowt_g = cs_scr[g], ecs_scr[g], rowt_scr[g]
        for d, (x_ref, b_ref, c_ref, y_ref) in enumerate(dirs):
            last = q - 1 if d == 0 else 0
            mask = (ri >= ci) if d == 0 else (ri <= ci)
            gcol = slice(g * STATE, (g + 1) * STATE)
            bg = b_ref[:, gcol]
            cg = c_ref[:, gcol]
            cb = lax.dot_general(cg, bg, (((1,), (1,)), ((), ())), preferred_element_type=F32)
            bg_t = bg.astype(F32).T.astype(BF16)
            for pair in range(HPG // 2):
                psl = slice(pair * LANES, (pair + 1) * LANES)
                s_pair = s_scr[d, g, :, psl]
                y_off = _dot(cg, s_pair.astype(BF16))
                col0 = g * (HPG * HEADDIM) + pair * LANES
                xp = x_ref[:, pl.ds(col0, LANES)]
                bd = _block_diag_pair(xp, masks)
                ms, bts, ecols, elasts = [], [], [], []
                for k in range(2):
                    hl = d * HPG + pair * 2 + k
                    col, rowt, clast = _head_terms(cs_g, rowt_g, hl, last)
                    colb = jnp.broadcast_to(col, (q, q))
                    ms.append((cb * jnp.exp2(jnp.where(mask, colb - rowt, -jnp.inf))).astype(BF16))
                    bts.append(bg_t * jnp.exp2(clast - rowt).astype(BF16))
                    ecols.append(jnp.exp2(colb))
                    elasts.append(jnp.exp2(clast))
                y = _dot(jnp.concatenate(ms, axis=1), bd)
                y = y + y_off * jnp.where(lane_lo, ecols[0], ecols[1])
                if d == 0:
                    y = y + xp.astype(F32) * dskip_ref[:, pl.ds(col0, LANES)]
                y_ref[:, pl.ds(col0, LANES)] = y.astype(BF16)
                grow = jnp.where(lane_lo_row, elasts[0], elasts[1])
                s_scr[d, g, :, psl] = s_pair * grow + _dot(jnp.concatenate(bts, axis=1), bd)
        return carry

    for g in range(GROUPS):
        group_body(g, 0)


def _ssd(p, dt, alog_row, dskip_row, h0, *, bsz, nc, x_blk, b_blk, c_blk):
    q = SSD_CHUNK
    t = p.shape[0]
    inner = GROUPS * HPG * HEADDIM
    gs = GROUPS * STATE

    def fwd(b, c):
        return b * nc + c

    def bwd(b, c):
        return b * nc + (nc - 1 - c)

    def specs(row):
        return [pl.BlockSpec((q, inner), lambda b, c: (row(b, c), x_blk)),
                pl.BlockSpec((q, gs), lambda b, c: (row(b, c), b_blk)),
                pl.BlockSpec((q, gs), lambda b, c: (row(b, c), c_blk)),
                pl.BlockSpec((q, LANES), lambda b, c: (row(b, c), 0))]

    return pl.pallas_call(
        _ssd_kernel,
        grid=(bsz, nc),
        in_specs=specs(fwd) + specs(bwd) + [
            pl.BlockSpec((1, LANES), lambda b, c: (0, 0)),
            pl.BlockSpec((1, inner), lambda b, c: (0, 0)),
            pl.BlockSpec((2, None, GROUPS, STATE, HPG * HEADDIM), lambda b, c: (0, b, 0, 0, 0))],
        out_specs=[pl.BlockSpec((q, inner), lambda b, c: (fwd(b, c), 0)),
                   pl.BlockSpec((q, inner), lambda b, c: (bwd(b, c), 0))],
        out_shape=[jax.ShapeDtypeStruct((t, inner), BF16), jax.ShapeDtypeStruct((t, inner), BF16)],
        scratch_shapes=[pltpu.VMEM((2, GROUPS, STATE, HPG * HEADDIM), F32),
                        pltpu.VMEM((GROUPS, q, GL), F32), pltpu.VMEM((GROUPS, q, GL), F32),
                        pltpu.VMEM((GROUPS, GL, q), F32)],
        compiler_params=_cparams(("arbitrary", "arbitrary")),
        name="ssd",
    )(p, p, p, dt, p, p, p, dt, alog_row, dskip_row, h0)


def _gnorm_kernel(yf_ref, yb_ref, z_ref, w_ref, o_ref):
    y = (yf_ref[...].astype(F32) + yb_ref[...].astype(F32)) * jax.nn.silu(z_ref[...].astype(F32))
    gw = y.shape[1] // GROUPS
    for g in range(GROUPS):
        yg = y[:, g * gw:(g + 1) * gw]
        ms = jnp.mean(yg * yg, axis=-1, keepdims=True)
        o_ref[:, g * gw:(g + 1) * gw] = (yg * lax.rsqrt(ms + EPS) * w_ref[:, g * gw:(g + 1) * gw]).astype(BF16)


def _gnorm(yf, yb, p, z_blk, norm_w, *, tm):
    t, inner = yf.shape
    return pl.pallas_call(
        _gnorm_kernel,
        grid=(t // tm,),
        in_specs=[pl.BlockSpec((tm, inner), lambda i: (i, 0)),
                  pl.BlockSpec((tm, inner), lambda i: (i, 0)),
                  pl.BlockSpec((tm, inner), lambda i: (i, z_blk)),
                  pl.BlockSpec((1, inner), lambda i: (0, 0))],
        out_specs=pl.BlockSpec((tm, inner), lambda i: (i, 0)),
        out_shape=jax.ShapeDtypeStruct((t, inner), BF16),
        compiler_params=_cparams(("parallel",)),
        name="gnorm",
    )(yf, yb, p, norm_w.reshape(1, inner))


def _merge_kernel(us_ref, yn_ref, ga_ref, gb_ref, ba_ref, bb_ref, wa_ref, wb_ref, o_ref):
    for s in range(o_ref.shape[1] // SUB):
        cols = slice(s * SUB, (s + 1) * SUB)
        ya = _dot(us_ref[...], wa_ref[:, cols])
        yb = _dot(yn_ref[...], wb_ref[:, cols])
        ga = jax.nn.sigmoid(ga_ref[:, cols].astype(F32) + ba_ref[:, cols])
        gb = jax.nn.sigmoid(gb_ref[:, cols].astype(F32) + bb_ref[:, cols])
        o_ref[:, cols] = (ga * ya + gb * yb).astype(BF16)


def _merge(us, yn, p, gate_col0, b_gate, wa, wb, *, tm, tn):
    t = us.shape[0]
    wa_in, d = wa.shape
    wb_in = wb.shape[0]
    ga_blk = gate_col0 // tn
    gb_blk = (gate_col0 + d) // tn
    return pl.pallas_call(
        _merge_kernel,
        grid=(t // tm, d // tn),
        in_specs=[pl.BlockSpec((tm, wa_in), lambda i, j: (i, 0)),
                  pl.BlockSpec((tm, wb_in), lambda i, j: (i, 0)),
                  pl.BlockSpec((tm, tn), lambda i, j: (i, ga_blk + j)),
                  pl.BlockSpec((tm, tn), lambda i, j: (i, gb_blk + j)),
                  pl.BlockSpec((1, tn), lambda i, j: (0, j)),
                  pl.BlockSpec((1, tn), lambda i, j: (0, d // tn + j)),
                  pl.BlockSpec((wa_in, tn), lambda i, j: (0, j)),
                  pl.BlockSpec((wb_in, tn), lambda i, j: (0, j))],
        out_specs=pl.BlockSpec((tm, tn), lambda i, j: (i, j)),
        out_shape=jax.ShapeDtypeStruct((t, d), BF16),
        compiler_params=_cparams(("parallel", "arbitrary")),
        name="merge",
    )(us, yn, p, p, b_gate, b_gate, wa, wb)


def _outproj_kernel(mg_ref, x_ref, m_ref, wo_ref, o_ref):
    for s in range(o_ref.shape[1] // OUT_SUB):
        cols = slice(s * OUT_SUB, (s + 1) * OUT_SUB)
        o_ref[:, cols] = x_ref[:, cols] + m_ref[:, cols] * _dot(mg_ref[...], wo_ref[:, cols])


def _outproj(mg, x2, mod3, mod_row, wo, *, tm):
    t, d = x2.shape
    return pl.pallas_call(
        _outproj_kernel,
        grid=(t // tm,),
        in_specs=[pl.BlockSpec((tm, d), lambda i: (i, 0)),
                  pl.BlockSpec((tm, d), lambda i: (i, 0)),
                  pl.BlockSpec((None, 1, d), lambda i: (mod_row(i), 0, 5)),
                  pl.BlockSpec((d, d), lambda i: (0, 0))],
        out_specs=pl.BlockSpec((tm, d), lambda i: (i, 0)),
        out_shape=jax.ShapeDtypeStruct((t, d), F32),
        compiler_params=_cparams(("parallel",)),
        name="outproj",
    )(mg, x2, mod3, wo)


def _dt_perm(a):
    lead = a.shape[:-2]
    a = a.reshape(*lead, 2, GROUPS, HPG)
    a = jnp.swapaxes(a, -3, -2)
    return a.reshape(*lead, 2 * GROUPS * HPG)


def kernel(x, c, ctx, c_ctx, w_mod, b_mod, norm_ffn1, ffn1_gate, ffn1_up, ffn1_down, norm_mix, w_in, b_gate,
           gmlp_ln_w, gmlp_ln_b, gmlp_ws, gmlp_bs, w_a, conv_w, conv_b, a_log, dt_bias, d_skip, ssm_norm,
           w_b, w_out, norm_ffn2, ffn2_gate, ffn2_up, ffn2_down, norm_final):
    bsz, seq, d = x.shape
    ctx_len = ctx.shape[1]
    depth = w_mod.shape[0]
    assert depth == 1, "context stream update between layers is not implemented"
    assert bsz < 8
    inner = GROUPS * HPG * HEADDIM
    gs = GROUPS * STATE
    a_width = w_a.shape[1]
    n_heads = GROUPS * HPG
    assert w_b.shape[1] == inner and conv_w.shape[2] == inner + 2 * gs
    assert w_in.shape[2] == 2 * a_width + 2 * inner + 2 * gs + 2 * n_heads + 2 * d

    t = bsz * seq
    tc = bsz * ctx_len
    x2 = x.reshape(t, d)
    ctx2 = ctx.reshape(tc, d)

    off_dt = 2 * a_width + 2 * inner + 2 * gs
    off_gate = off_dt + 2 * n_heads
    n_gelu, n_plain, n_conv = 2 * a_width, inner, inner + 2 * gs
    col_z = n_gelu
    col_x = n_gelu + n_plain
    col_b = col_x + inner
    col_c = col_b + gs
    col_gate = n_gelu + n_plain + n_conv

    i = 0
    w_all = w_in[i].astype(BF16)
    w_dt = _dt_perm(w_in[i][:, off_dt:off_gate].reshape(d, 2, n_heads)).astype(BF16)
    dt_bias_row = _dt_perm(dt_bias[i]).reshape(1, 2 * n_heads)
    alog_row = _dt_perm(a_log[i].astype(F32)).reshape(1, 2 * n_heads)
    dskip_row = jnp.repeat(d_skip[i], HEADDIM).reshape(1, inner)
    bf = lambda w: w.astype(BF16)

    c8 = jnp.zeros((8, d), F32).at[:bsz].set(c).at[bsz].set(c_ctx)
    mod3 = _adaln(c8, w_mod[i], b_mod[i]).reshape(8, 1, N_MOD * d)

    tm = 512
    tm_proj = 512

    def lat_row(rows):
        assert seq % rows == 0, "a token tile must not straddle two samples"
        return lambda ti: (ti * rows) // seq

    ctx_row = lambda ti: bsz

    ffn1_w = (bf(ffn1_gate[i]), bf(ffn1_up[i]), bf(ffn1_down[i]))
    x1 = _ffn(x2, mod3, 0, lat_row(tm), norm_ffn1[i], *ffn1_w, None, tm=tm, tf=512)
    ctx1 = _ffn(ctx2, mod3, 0, ctx_row, norm_ffn1[i], *ffn1_w, None, tm=tm, tf=512)

    tn_proj = 2048
    proj_args = (norm_mix[i], w_all, w_dt, dt_bias_row, conv_w[i], conv_b[i].reshape(1, -1))
    proj_kw = dict(tm=tm_proj, tn=tn_proj, n_gelu=n_gelu, n_plain=n_plain, n_conv=n_conv, n_gate=2 * d)
    ctx_tiles = (col_x // tn_proj, n_conv // tn_proj)
    pc, dtc = _inproj(ctx1, mod3, ctx_row, *proj_args, seg=ctx_len, tiles=ctx_tiles, **proj_kw)
    h0 = _ctx_states(pc, dtc, alog_row, bsz=bsz, q=ctx_len, x_blk=0, b_blk=inner // gs)

    p, dt = _inproj(x1, mod3, lat_row(tm_proj), *proj_args, seg=GRID_W, **proj_kw)
    us = _gmlp(p, a_width, gmlp_ln_w[i], gmlp_ln_b[i], bf(gmlp_ws[i]), gmlp_bs[i].T, tm=tm)
    yf, yb = _ssd(p, dt, alog_row, dskip_row, h0, bsz=bsz, nc=seq // SSD_CHUNK,
                  x_blk=col_x // inner, b_blk=col_b // gs, c_blk=col_c // gs)
    yn = _gnorm(yf, yb, p, col_z // inner, ssm_norm[i], tm=256)
    mg = _merge(us, yn, p, col_gate, b_gate[i].reshape(1, -1), bf(w_a[i]), bf(w_b[i]), tm=1024, tn=512)
    x3 = _outproj(mg, x1, mod3, lat_row(tm), bf(w_out[i]), tm=tm)

    out = _ffn(x3, mod3, 6, lat_row(tm), norm_ffn2[i], bf(ffn2_gate[i]), bf(ffn2_up[i]), bf(ffn2_down[i]),
               norm_final, tm=tm, tf=512)
    return out.reshape(bsz, seq, d)
```

```python
import functools

import jax
import jax.numpy as jnp
from jax import lax
from jax.experimental import pallas as pl
from jax.experimental.pallas import tpu as pltpu

F32 = jnp.float32
BF16 = jnp.bfloat16
EPS = 1e-6
LOG2E = 1.4426950408889634

GRID_W = 64
A_GROUPS = 8
A_CHUNK = 128
HEADDIM = 64
GROUPS = 8
STATE = 128
CONV_K = 5
SSD_CHUNK = 128
N_MOD = 9

LANES = 128
SUB = 256
OUT_SUB = 512
PROJ_SUB = 512
PROJ_CHUNKS = (512, 512, 512, 256, 256)
SUBLANES = 8
BF16_ROWS = 16
VMEM_LIMIT = 56 * 1024 * 1024


def _cparams(sem):
    return pltpu.CompilerParams(dimension_semantics=sem, vmem_limit_bytes=VMEM_LIMIT)


def _dot(a, b):
    return jnp.dot(a, b, preferred_element_type=F32)


def _split3(a):
    a1 = a.astype(BF16)
    r1 = a - a1.astype(F32)
    a2 = r1.astype(BF16)
    r2 = r1 - a2.astype(F32)
    return a1, a2, r2.astype(BF16)


def _dot_exact_lhs(tri, a):
    a1, a2, a3 = _split3(a)
    return _dot(tri, a1) + _dot(tri, a2) + _dot(tri, a3)


def _dot_exact_rhs(a, tri):
    a1, a2, a3 = _split3(a)
    return _dot(a1, tri) + _dot(a2, tri) + _dot(a3, tri)


def _rms_mod_store(x_ref, h_ref, nw, scale, shift):
    gain = nw * (1.0 + scale)
    for r in range(0, x_ref.shape[0], BF16_ROWS):
        x = x_ref[r:r + BF16_ROWS, :]
        ms = jnp.mean(x * x, axis=-1, keepdims=True)
        h_ref[r:r + BF16_ROWS, :] = (x * lax.rsqrt(ms + EPS) * gain + shift).astype(BF16)


def _adaln_kernel(c_ref, w_ref, b_ref, o_ref):
    s = jax.nn.silu(c_ref[...])
    s1, s2, s3 = _split3(s)
    w = w_ref[...].astype(BF16)
    o_ref[...] = _dot(s1, w) + _dot(s2, w) + _dot(s3, w) + b_ref[...]


def _adaln(c8, w_mod, b_mod):
    d, n = w_mod.shape
    tn = 1024
    return pl.pallas_call(
        _adaln_kernel,
        grid=(n // tn,),
        in_specs=[pl.BlockSpec((8, d), lambda j: (0, 0)),
                  pl.BlockSpec((d, tn), lambda j: (0, j)),
                  pl.BlockSpec((1, tn), lambda j: (0, j))],
        out_specs=pl.BlockSpec((8, tn), lambda j: (0, j)),
        out_shape=jax.ShapeDtypeStruct((8, n), F32),
        compiler_params=_cparams(("arbitrary",)),
        name="adaln",
    )(c8, w_mod, b_mod.reshape(1, n))


def _ffn_kernel(x_ref, sh_ref, sc_ref, gt_ref, nw_ref, wg_ref, wu_ref, wd_ref, fw_ref, o_ref,
                h_scr, acc_scr, *, final_norm):
    j = pl.program_id(1)

    @pl.when(j == 0)
    def _():
        _rms_mod_store(x_ref, h_scr, nw_ref[...], sc_ref[...], sh_ref[...])
        acc_scr[...] = jnp.zeros_like(acc_scr)

    acts = []
    for s in range(wg_ref.shape[1] // SUB):
        cols = slice(s * SUB, (s + 1) * SUB)
        g = _dot(h_scr[...], wg_ref[:, cols])
        u = _dot(h_scr[...], wu_ref[:, cols])
        acts.append((jax.nn.silu(g) * u).astype(BF16))
    acc_scr[...] += _dot(jnp.concatenate(acts, axis=1), wd_ref[...])

    @pl.when(j == pl.num_programs(1) - 1)
    def _():
        half_gate = 0.5 * gt_ref[...]
        for r in range(0, x_ref.shape[0], SUBLANES):
            rows = slice(r, r + SUBLANES)
            out = x_ref[rows, :] + half_gate * acc_scr[rows, :]
            if final_norm:
                ms = jnp.mean(out * out, axis=-1, keepdims=True)
                out = out * lax.rsqrt(ms + EPS) * fw_ref[...]
            o_ref[rows, :] = out


def _ffn(x2, mod3, mod_k, mod_row, norm_w, wg, wu, wd, final_w, *, tm, tf):
    t, d = x2.shape
    f = wg.shape[1]
    final_norm = final_w is not None
    fw = final_w if final_norm else norm_w

    def mod_spec(k):
        return pl.BlockSpec((None, 1, d), lambda i, j: (mod_row(i), 0, k))

    return pl.pallas_call(
        functools.partial(_ffn_kernel, final_norm=final_norm),
        grid=(t // tm, f // tf),
        in_specs=[pl.BlockSpec((tm, d), lambda i, j: (i, 0)),
                  mod_spec(mod_k), mod_spec(mod_k + 1), mod_spec(mod_k + 2),
                  pl.BlockSpec((1, d), lambda i, j: (0, 0)),
                  pl.BlockSpec((d, tf), lambda i, j: (0, j)),
                  pl.BlockSpec((d, tf), lambda i, j: (0, j)),
                  pl.BlockSpec((tf, d), lambda i, j: (j, 0)),
                  pl.BlockSpec((1, d), lambda i, j: (0, 0))],
        out_specs=pl.BlockSpec((tm, d), lambda i, j: (i, 0)),
        out_shape=jax.ShapeDtypeStruct((t, d), F32),
        scratch_shapes=[pltpu.VMEM((tm, d), BF16), pltpu.VMEM((tm, d), F32)],
        compiler_params=_cparams(("parallel", "arbitrary")),
        name="ffn",
    )(x2, mod3, mod3, mod3, norm_w.reshape(1, d), wg, wu, wd, fw.reshape(1, d))


def _shift_in_segment(x4, off):
    nseg, _, sub, c = x4.shape
    r = pltpu.roll(x4, (-off) % sub, axis=2)
    zero = jnp.zeros((nseg, 1, sub, c), x4.dtype)
    row = lax.broadcasted_iota(jnp.int32, (1, 1, sub, c), 2)
    if off < 0:
        return jnp.where(row < -off, jnp.concatenate([zero, r[:, :-1]], axis=1), r)
    return jnp.where(row >= sub - off, jnp.concatenate([r[:, 1:], zero], axis=1), r)


def _inproj_kernel(x_ref, sh_ref, sc_ref, nw_ref, w_ref, wc_ref, wg_ref, wdt_ref, dtb_ref, cw_ref, cb_ref,
                   p_ref, dt_ref, h_scr, acc_scr, *, seg, j0, j_gelu, j_plain, j_mix):
    j = pl.program_id(1) + j0
    tm, tn = p_ref.shape

    @pl.when(j == j0)
    def _():
        _rms_mod_store(x_ref, h_scr, nw_ref[...], sc_ref[...], sh_ref[...])
        dt_ref[...] = jax.nn.softplus(_dot(h_scr[...], wdt_ref[...]) + dtb_ref[...])

    def run(chunks):
        def finish(s):
            _, wcols, ocols, epilogue = chunks[s]
            p_ref[:, ocols] = epilogue(acc_scr[s % 2, :, :ocols.stop - ocols.start], wcols).astype(BF16)

        for s, (wr, wcols, ocols, _) in enumerate(chunks):
            if s > 0:
                finish(s - 1)
            acc_scr[s % 2, :, :ocols.stop - ocols.start] = _dot(h_scr[...], wr[:, wcols])
        finish(len(chunks) - 1)

    def uniform(epilogue):
        assert sum(PROJ_CHUNKS) == tn
        starts = [sum(PROJ_CHUNKS[:k]) for k in range(len(PROJ_CHUNKS))]
        return [(w_ref, slice(a, a + w), slice(a, a + w), epilogue) for a, w in zip(starts, PROJ_CHUNKS)]

    def plain(acc, cols):
        return acc

    def conv_silu(acc, cols):
        pad = (CONV_K - 1) // 2
        x4 = acc.reshape(tm // seg, seg // SUBLANES, SUBLANES, acc.shape[1])
        y = cb_ref[:, cols] + cw_ref[pad:pad + 1, cols] * x4
        for tap in range(CONV_K):
            if tap != pad:
                y = y + cw_ref[tap:tap + 1, cols] * _shift_in_segment(x4, tap - pad)
        return jax.nn.silu(y).reshape(acc.shape)

    @pl.when(j < j_gelu)
    def _():
        run(uniform(lambda acc, cols: jax.nn.gelu(acc)))

    @pl.when(((j >= j_gelu) & (j < j_plain)) | (j >= j_mix))
    def _():
        run(uniform(plain))

    @pl.when((j >= j_plain) & (j < j_mix))
    def _():
        n_conv_chunks = wc_ref.shape[1] // PROJ_SUB
        chunks = [(wc_ref, slice(k * PROJ_SUB, (k + 1) * PROJ_SUB), slice(k * PROJ_SUB, (k + 1) * PROJ_SUB), conv_silu)
                  for k in range(n_conv_chunks)]
        chunks.append((wg_ref, slice(0, PROJ_SUB), slice(n_conv_chunks * PROJ_SUB, tn), plain))
        run(chunks)


def _inproj(x2, mod3, mod_row, norm_w, w_all, w_dt, dt_bias, conv_w, conv_b, *, seg, tm, tn,
            n_gelu, n_plain, n_conv, n_gate, tiles=None):
    t, d = x2.shape
    lay = _proj_layout(tn, n_gelu, n_plain, n_conv, n_gate)
    j_gelu, j_plain, j_mix, n_tiles, conv_w_tile, n_mix = (lay[k] for k in
                                                         ("j_gelu", "j_plain", "j_mix", "n_tiles", "conv_w_tile", "n_mix"))
    n_dt = w_dt.shape[1]
    w_conv0 = n_gelu + n_plain
    w_gate0 = w_conv0 + n_conv + n_dt
    j0, nj = tiles if tiles is not None else (0, n_tiles)
    assert t % tm == 0 and tm % seg == 0 and w_all.shape[1] == w_gate0 + n_gate

    def mod_spec(k):
        return pl.BlockSpec((None, 1, d), lambda i, j: (mod_row(i), 0, k))

    def mix(j):
        return jnp.clip(j + j0 - j_plain, 0, n_mix - 1)

    def w_col(j):
        jj = j + j0
        tail = w_gate0 + n_mix * PROJ_SUB + (jj - j_mix) * tn
        return pl.multiple_of(jnp.where(jj < j_plain, jj * tn, jnp.where(jj < j_mix, (j_plain - 1) * tn, tail)), LANES)

    def elem(rows, cols, col_fn):
        return pl.BlockSpec((pl.Element(rows), pl.Element(cols)), lambda i, j: (0, pl.multiple_of(col_fn(j), LANES)))

    return pl.pallas_call(
        functools.partial(_inproj_kernel, seg=seg, j0=j0, j_gelu=j_gelu, j_plain=j_plain, j_mix=j_mix),
        grid=(t // tm, nj),
        in_specs=[pl.BlockSpec((tm, d), lambda i, j: (i, 0)),
                  mod_spec(3), mod_spec(4),
                  pl.BlockSpec((1, d), lambda i, j: (0, 0)),
                  elem(d, tn, w_col),
                  elem(d, conv_w_tile, lambda j: w_conv0 + mix(j) * conv_w_tile),
                  elem(d, PROJ_SUB, lambda j: w_gate0 + mix(j) * PROJ_SUB),
                  pl.BlockSpec((d, LANES), lambda i, j: (0, 0)),
                  pl.BlockSpec((1, LANES), lambda i, j: (0, 0)),
                  elem(CONV_K, conv_w_tile, lambda j: mix(j) * conv_w_tile),
                  elem(1, conv_w_tile, lambda j: mix(j) * conv_w_tile)],
        out_specs=[pl.BlockSpec((tm, tn), lambda i, j: (i, j)),
                   pl.BlockSpec((tm, LANES), lambda i, j: (i, 0))],
        out_shape=[jax.ShapeDtypeStruct((t, nj * tn), BF16), jax.ShapeDtypeStruct((t, LANES), F32)],
        scratch_shapes=[pltpu.VMEM((tm, d), BF16), pltpu.VMEM((2, tm, PROJ_SUB), F32)],
        compiler_params=_cparams(("parallel", "arbitrary")),
        name="inproj",
    )(x2, mod3, mod3, norm_w.reshape(1, d), w_all, w_all, w_all, w_dt, dt_bias, conv_w, conv_b)


def _proj_layout(tn, n_gelu, n_plain, n_conv, n_gate):
    conv_w_tile = tn - PROJ_SUB
    assert n_gelu % tn == 0 and n_plain % tn == 0 and n_conv % conv_w_tile == 0
    n_mix = n_conv // conv_w_tile
    gate_left = n_gate - n_mix * PROJ_SUB
    assert gate_left >= 0 and gate_left % tn == 0
    j_gelu = n_gelu // tn
    j_plain = j_gelu + n_plain // tn
    j_mix = j_plain + n_mix
    mix0 = j_plain * tn

    def conv_col(c):
        return mix0 + (c // conv_w_tile) * tn + c % conv_w_tile

    def gate_col(k):
        return mix0 + k * tn + conv_w_tile if k < n_mix else j_mix * tn + (k - n_mix) * PROJ_SUB

    return dict(j_gelu=j_gelu, j_plain=j_plain, j_mix=j_mix, n_mix=n_mix, n_tiles=j_mix + gate_left // tn,
                conv_w_tile=conv_w_tile, mix0=mix0, mix_width=n_mix * tn, conv_col=conv_col, gate_col=gate_col)


def _gmlp_kernel(u_ref, v_ref, lw_ref, lb_ref, ws_ref, bs_ref, o_ref):
    v = v_ref[...].astype(F32)
    mu = jnp.mean(v, axis=-1, keepdims=True)
    vc = v - mu
    var = jnp.mean(vc * vc, axis=-1, keepdims=True)
    vn = (vc * lax.rsqrt(var + EPS) * lw_ref[...] + lb_ref[...]).astype(BF16)
    tm, width = vn.shape
    gw = width // A_GROUPS
    for c in range(tm // A_CHUNK):
        r0 = c * A_CHUNK
        for g in range(A_GROUPS):
            c0 = g * gw
            s = _dot(ws_ref[g], vn[r0:r0 + A_CHUNK, c0:c0 + gw]) + bs_ref[:, g:g + 1]
            u = u_ref[r0:r0 + A_CHUNK, c0:c0 + gw].astype(F32)
            o_ref[r0:r0 + A_CHUNK, c0:c0 + gw] = (u * s).astype(BF16)


def _gmlp(p, width, ln_w, ln_b, ws, bs_t, *, tm):
    t = p.shape[0]
    return pl.pallas_call(
        _gmlp_kernel,
        grid=(t // tm,),
        in_specs=[pl.BlockSpec((tm, width), lambda i: (i, 0)),
                  pl.BlockSpec((tm, width), lambda i: (i, 1)),
                  pl.BlockSpec((1, width), lambda i: (0, 0)),
                  pl.BlockSpec((1, width), lambda i: (0, 0)),
                  pl.BlockSpec((A_GROUPS, A_CHUNK, A_CHUNK), lambda i: (0, 0, 0)),
                  pl.BlockSpec((A_CHUNK, A_GROUPS), lambda i: (0, 0))],
        out_specs=pl.BlockSpec((tm, width), lambda i: (i, 0)),
        out_shape=jax.ShapeDtypeStruct((t, width), BF16),
        compiler_params=_cparams(("parallel",)),
        name="gmlp",
    )(p, p, ln_w.reshape(1, width), ln_b.reshape(1, width), ws, bs_t)


HPG = 8
GL = 2 * HPG


def _decay_tables(dt, alog_row):
    q = dt.shape[0]
    a = dt * (-jnp.exp(alog_row) * LOG2E)
    ri = lax.broadcasted_iota(jnp.int32, (q, q), 0)
    ci = lax.broadcasted_iota(jnp.int32, (q, q), 1)
    tril = jnp.where(ri >= ci, 1.0, 0.0).astype(BF16)
    triu = jnp.where(ri <= ci, 1.0, 0.0).astype(BF16)
    lane_f = (lax.broadcasted_iota(jnp.int32, (q, LANES), 1) % GL) < HPG
    sub_f = (lax.broadcasted_iota(jnp.int32, (LANES, q), 0) % GL) < HPG
    cs = jnp.where(lane_f, _dot_exact_lhs(tril, a), _dot_exact_lhs(triu, a))
    a_t = a.T
    cst = jnp.where(sub_f, _dot_exact_rhs(a_t, triu), _dot_exact_rhs(a_t, tril))
    return cs, cst - jnp.log2(dt.T)


def _pair_masks(q):
    lane = lax.broadcasted_iota(jnp.int32, (q, LANES), 1)
    return (jnp.where(lane < HEADDIM, 1.0, 0.0).astype(BF16),
            jnp.where(lane >= HEADDIM, 1.0, 0.0).astype(BF16))


def _block_diag_pair(xp, masks):
    return jnp.concatenate([xp * masks[0], xp * masks[1]], axis=0)


def _head_terms(cs_ref, rowt_ref, hl, last):
    q = cs_ref.shape[0]
    col = cs_ref[:, hl:hl + 1]
    rowt = rowt_ref[hl:hl + 1, :]
    clast = jnp.broadcast_to(col[last:last + 1, :], (1, q))
    return col, rowt, clast


def _ctx_state_kernel(p_ref, dt_ref, alog_ref, h_ref, cs_scr, rowt_scr, *, loc):
    q = p_ref.shape[0]
    cs_scr[...], rowt_scr[...] = _decay_tables(dt_ref[...], alog_ref[...])
    masks = _pair_masks(q)
    inner = GROUPS * HPG * HEADDIM
    for g in range(GROUPS):
        bcol = loc(inner + g * STATE)
        bg_t = p_ref[:, bcol:bcol + STATE].astype(F32).T.astype(BF16)
        for d in range(2):
            last = q - 1 if d == 0 else 0
            for pair in range(HPG // 2):
                xcol = loc(g * (HPG * HEADDIM) + pair * LANES)
                bd = _block_diag_pair(p_ref[:, xcol:xcol + LANES], masks)
                bts = []
                for k in range(2):
                    _, rowt, clast = _head_terms(cs_scr, rowt_scr, g * GL + d * HPG + pair * 2 + k, last)
                    bts.append(bg_t * jnp.exp2(clast - rowt).astype(BF16))
                h_ref[d, g, :, pair * LANES:(pair + 1) * LANES] = _dot(jnp.concatenate(bts, axis=1), bd)


def _ctx_states(pc, dtc, alog_row, loc, *, bsz, q):
    width = pc.shape[1]
    return pl.pallas_call(
        functools.partial(_ctx_state_kernel, loc=loc),
        grid=(bsz,),
        in_specs=[pl.BlockSpec((q, width), lambda b: (b, 0)),
                  pl.BlockSpec((q, LANES), lambda b: (b, 0)),
                  pl.BlockSpec((1, LANES), lambda b: (0, 0))],
        out_specs=pl.BlockSpec((2, None, GROUPS, STATE, HPG * HEADDIM), lambda b: (0, b, 0, 0, 0)),
        out_shape=jax.ShapeDtypeStruct((2, bsz, GROUPS, STATE, HPG * HEADDIM), F32),
        scratch_shapes=[pltpu.VMEM((q, LANES), F32), pltpu.VMEM((LANES, q), F32)],
        compiler_params=_cparams(("arbitrary",)),
        name="ctx_state",
    )(pc, dtc, alog_row)


def _ssd_kernel(pf_ref, dtf_ref, pb_ref, dtb_ref, alog_ref, dskip_ref, h0_ref,
                yf_ref, yb_ref, s_scr, cs_scr, rowt_scr, *, loc):
    q = pf_ref.shape[0]
    inner = GROUPS * HPG * HEADDIM

    @pl.when(pl.program_id(1) == 0)
    def _():
        s_scr[...] = h0_ref[...]

    lane_f = (lax.broadcasted_iota(jnp.int32, (q, LANES), 1) % GL) < HPG
    dt = jnp.where(lane_f, dtf_ref[...], dtb_ref[...])
    cs_scr[...], rowt_scr[...] = _decay_tables(dt, alog_ref[...])

    ri = lax.broadcasted_iota(jnp.int32, (q, q), 0)
    ci = lax.broadcasted_iota(jnp.int32, (q, q), 1)
    lane_lo = lax.broadcasted_iota(jnp.int32, (q, LANES), 1) < HEADDIM
    lane_lo_row = lax.broadcasted_iota(jnp.int32, (1, LANES), 1) < HEADDIM
    masks = _pair_masks(q)
    dirs = ((pf_ref, yf_ref), (pb_ref, yb_ref))

    def group_body(g, carry):
        for d, (p_ref, y_ref) in enumerate(dirs):
            last = q - 1 if d == 0 else 0
            mask = (ri >= ci) if d == 0 else (ri <= ci)
            bcol, ccol = loc(inner + g * STATE), loc(inner + GROUPS * STATE + g * STATE)
            bg = p_ref[:, bcol:bcol + STATE]
            cg = p_ref[:, ccol:ccol + STATE]
            cb = lax.dot_general(cg, bg, (((1,), (1,)), ((), ())), preferred_element_type=F32)
            bg_t = bg.astype(F32).T.astype(BF16)
            for pair in range(HPG // 2):
                psl = slice(pair * LANES, (pair + 1) * LANES)
                s_pair = s_scr[d, g, :, psl]
                y_off = _dot(cg, s_pair.astype(BF16))
                col0 = g * (HPG * HEADDIM) + pair * LANES
                xp = p_ref[:, pl.ds(loc(col0), LANES)]
                bd = _block_diag_pair(xp, masks)
                ms, bts, ecols, elasts = [], [], [], []
                for k in range(2):
                    hl = g * GL + d * HPG + pair * 2 + k
                    col, rowt, clast = _head_terms(cs_scr, rowt_scr, hl, last)
                    colb = jnp.broadcast_to(col, (q, q))
                    ms.append((cb * jnp.exp2(jnp.where(mask, colb - rowt, -jnp.inf))).astype(BF16))
                    bts.append(bg_t * jnp.exp2(clast - rowt).astype(BF16))
                    ecols.append(jnp.exp2(colb))
                    elasts.append(jnp.exp2(clast))
                y = _dot(jnp.concatenate(ms, axis=1), bd)
                y = y + y_off * jnp.where(lane_lo, ecols[0], ecols[1])
                if d == 0:
                    y = y + xp.astype(F32) * dskip_ref[:, pl.ds(col0, LANES)]
                y_ref[:, pl.ds(col0, LANES)] = y.astype(BF16)
                grow = jnp.where(lane_lo_row, elasts[0], elasts[1])
                s_scr[d, g, :, psl] = s_pair * grow + _dot(jnp.concatenate(bts, axis=1), bd)
        return carry

    for g in range(GROUPS):
        group_body(g, 0)


def _ssd(p, dt, alog_row, dskip_row, h0, loc, *, bsz, nc, p_blk, p_width):
    q = SSD_CHUNK
    t = p.shape[0]
    inner = GROUPS * HPG * HEADDIM

    def fwd(b, c):
        return b * nc + c

    def bwd(b, c):
        return b * nc + (nc - 1 - c)

    def specs(row):
        return [pl.BlockSpec((q, p_width), lambda b, c: (row(b, c), p_blk)),
                pl.BlockSpec((q, LANES), lambda b, c: (row(b, c), 0))]

    return pl.pallas_call(
        functools.partial(_ssd_kernel, loc=loc),
        grid=(bsz, nc),
        in_specs=specs(fwd) + specs(bwd) + [
            pl.BlockSpec((1, LANES), lambda b, c: (0, 0)),
            pl.BlockSpec((1, inner), lambda b, c: (0, 0)),
            pl.BlockSpec((2, None, GROUPS, STATE, HPG * HEADDIM), lambda b, c: (0, b, 0, 0, 0))],
        out_specs=[pl.BlockSpec((q, inner), lambda b, c: (fwd(b, c), 0)),
                   pl.BlockSpec((q, inner), lambda b, c: (bwd(b, c), 0))],
        out_shape=[jax.ShapeDtypeStruct((t, inner), BF16), jax.ShapeDtypeStruct((t, inner), BF16)],
        scratch_shapes=[pltpu.VMEM((2, GROUPS, STATE, HPG * HEADDIM), F32),
                        pltpu.VMEM((q, LANES), F32), pltpu.VMEM((LANES, q), F32)],
        compiler_params=_cparams(("arbitrary", "arbitrary")),
        name="ssd",
    )(p, dt, p, dt, alog_row, dskip_row, h0)


def _gnorm_kernel(yf_ref, yb_ref, z_ref, w_ref, o_ref):
    y = (yf_ref[...].astype(F32) + yb_ref[...].astype(F32)) * jax.nn.silu(z_ref[...].astype(F32))
    gw = y.shape[1] // GROUPS
    for g in range(GROUPS):
        yg = y[:, g * gw:(g + 1) * gw]
        ms = jnp.mean(yg * yg, axis=-1, keepdims=True)
        o_ref[:, g * gw:(g + 1) * gw] = (yg * lax.rsqrt(ms + EPS) * w_ref[:, g * gw:(g + 1) * gw]).astype(BF16)


def _gnorm(yf, yb, p, z_blk, norm_w, *, tm):
    t, inner = yf.shape
    return pl.pallas_call(
        _gnorm_kernel,
        grid=(t // tm,),
        in_specs=[pl.BlockSpec((tm, inner), lambda i: (i, 0)),
                  pl.BlockSpec((tm, inner), lambda i: (i, 0)),
                  pl.BlockSpec((tm, inner), lambda i: (i, z_blk)),
                  pl.BlockSpec((1, inner), lambda i: (0, 0))],
        out_specs=pl.BlockSpec((tm, inner), lambda i: (i, 0)),
        out_shape=jax.ShapeDtypeStruct((t, inner), BF16),
        compiler_params=_cparams(("parallel",)),
        name="gnorm",
    )(yf, yb, p, norm_w.reshape(1, inner))


def _merge_kernel(us_ref, yn_ref, ga_ref, gb_ref, ba_ref, bb_ref, wa_ref, wb_ref, o_ref):
    for s in range(o_ref.shape[1] // SUB):
        cols = slice(s * SUB, (s + 1) * SUB)
        ya = _dot(us_ref[...], wa_ref[:, cols])
        yb = _dot(yn_ref[...], wb_ref[:, cols])
        ga = jax.nn.sigmoid(ga_ref[:, cols].astype(F32) + ba_ref[:, cols])
        gb = jax.nn.sigmoid(gb_ref[:, cols].astype(F32) + bb_ref[:, cols])
        o_ref[:, cols] = (ga * ya + gb * yb).astype(BF16)


def _merge(us, yn, p, lay, b_gate, wa, wb, *, tm):
    t = us.shape[0]
    wa_in, d = wa.shape
    wb_in = wb.shape[0]
    tn = PROJ_SUB
    n_mix, tn_proj = lay["n_mix"], lay["mix_width"] // lay["n_mix"]

    def gate_blk(k):
        in_mix = (lay["mix0"] + k * tn_proj + lay["conv_w_tile"]) // tn
        tail = (lay["j_mix"] * tn_proj) // tn + (k - n_mix)
        return jnp.where(k < n_mix, in_mix, tail)

    return pl.pallas_call(
        _merge_kernel,
        grid=(t // tm, d // tn),
        in_specs=[pl.BlockSpec((tm, wa_in), lambda i, j: (i, 0)),
                  pl.BlockSpec((tm, wb_in), lambda i, j: (i, 0)),
                  pl.BlockSpec((tm, tn), lambda i, j: (i, gate_blk(j))),
                  pl.BlockSpec((tm, tn), lambda i, j: (i, gate_blk(d // tn + j))),
                  pl.BlockSpec((1, tn), lambda i, j: (0, j)),
                  pl.BlockSpec((1, tn), lambda i, j: (0, d // tn + j)),
                  pl.BlockSpec((wa_in, tn), lambda i, j: (0, j)),
                  pl.BlockSpec((wb_in, tn), lambda i, j: (0, j))],
        out_specs=pl.BlockSpec((tm, tn), lambda i, j: (i, j)),
        out_shape=jax.ShapeDtypeStruct((t, d), BF16),
        compiler_params=_cparams(("parallel", "arbitrary")),
        name="merge",
    )(us, yn, p, p, b_gate, b_gate, wa, wb)


def _outproj_kernel(mg_ref, x_ref, m_ref, wo_ref, o_ref):
    for s in range(o_ref.shape[1] // OUT_SUB):
        cols = slice(s * OUT_SUB, (s + 1) * OUT_SUB)
        o_ref[:, cols] = x_ref[:, cols] + m_ref[:, cols] * _dot(mg_ref[...], wo_ref[:, cols])


def _outproj(mg, x2, mod3, mod_row, wo, *, tm):
    t, d = x2.shape
    return pl.pallas_call(
        _outproj_kernel,
        grid=(t // tm,),
        in_specs=[pl.BlockSpec((tm, d), lambda i: (i, 0)),
                  pl.BlockSpec((tm, d), lambda i: (i, 0)),
                  pl.BlockSpec((None, 1, d), lambda i: (mod_row(i), 0, 5)),
                  pl.BlockSpec((d, d), lambda i: (0, 0))],
        out_specs=pl.BlockSpec((tm, d), lambda i: (i, 0)),
        out_shape=jax.ShapeDtypeStruct((t, d), F32),
        compiler_params=_cparams(("parallel",)),
        name="outproj",
    )(mg, x2, mod3, wo)


def _dt_perm(a):
    lead = a.shape[:-2]
    a = a.reshape(*lead, 2, GROUPS, HPG)
    a = jnp.swapaxes(a, -3, -2)
    return a.reshape(*lead, 2 * GROUPS * HPG)


def kernel(x, c, ctx, c_ctx, w_mod, b_mod, norm_ffn1, ffn1_gate, ffn1_up, ffn1_down, norm_mix, w_in, b_gate,
           gmlp_ln_w, gmlp_ln_b, gmlp_ws, gmlp_bs, w_a, conv_w, conv_b, a_log, dt_bias, d_skip, ssm_norm,
           w_b, w_out, norm_ffn2, ffn2_gate, ffn2_up, ffn2_down, norm_final):
    bsz, seq, d = x.shape
    ctx_len = ctx.shape[1]
    depth = w_mod.shape[0]
    assert depth == 1, "context stream update between layers is not implemented"
    assert bsz < 8
    inner = GROUPS * HPG * HEADDIM
    gs = GROUPS * STATE
    a_width = w_a.shape[1]
    n_heads = GROUPS * HPG
    assert w_b.shape[1] == inner and conv_w.shape[2] == inner + 2 * gs
    assert w_in.shape[2] == 2 * a_width + 2 * inner + 2 * gs + 2 * n_heads + 2 * d

    t = bsz * seq
    tc = bsz * ctx_len
    x2 = x.reshape(t, d)
    ctx2 = ctx.reshape(tc, d)

    off_dt = 2 * a_width + 2 * inner + 2 * gs
    off_gate = off_dt + 2 * n_heads
    n_gelu, n_plain, n_conv = 2 * a_width, inner, inner + 2 * gs
    col_z = n_gelu

    i = 0
    w_all = w_in[i].astype(BF16)
    w_dt = _dt_perm(w_in[i][:, off_dt:off_gate].reshape(d, 2, n_heads)).astype(BF16)
    dt_bias_row = _dt_perm(dt_bias[i]).reshape(1, 2 * n_heads)
    alog_row = _dt_perm(a_log[i].astype(F32)).reshape(1, 2 * n_heads)
    dskip_row = jnp.repeat(d_skip[i], HEADDIM).reshape(1, inner)
    bf = lambda w: w.astype(BF16)

    c8 = jnp.zeros((8, d), F32).at[:bsz].set(c).at[bsz].set(c_ctx)
    mod3 = _adaln(c8, w_mod[i], b_mod[i]).reshape(8, 1, N_MOD * d)

    tm = 512
    tm_proj = 512

    def lat_row(rows):
        assert seq % rows == 0, "a token tile must not straddle two samples"
        return lambda ti: (ti * rows) // seq

    ctx_row = lambda ti: bsz

    ffn1_w = (bf(ffn1_gate[i]), bf(ffn1_up[i]), bf(ffn1_down[i]))
    x1 = _ffn(x2, mod3, 0, lat_row(tm), norm_ffn1[i], *ffn1_w, None, tm=tm, tf=512)
    ctx1 = _ffn(ctx2, mod3, 0, ctx_row, norm_ffn1[i], *ffn1_w, None, tm=tm, tf=512)

    tn_proj = 2048
    proj_args = (norm_mix[i], w_all, w_dt, dt_bias_row, conv_w[i], conv_b[i].reshape(1, -1))
    proj_kw = dict(tm=tm_proj, tn=tn_proj, n_gelu=n_gelu, n_plain=n_plain, n_conv=n_conv, n_gate=2 * d)
    lay = _proj_layout(tn_proj, n_gelu, n_plain, n_conv, 2 * d)
    assert lay["mix0"] % lay["mix_width"] == 0
    conv_loc = lambda ch: lay["conv_col"](ch) - lay["mix0"]
    pc, dtc = _inproj(ctx1, mod3, ctx_row, *proj_args, seg=ctx_len, tiles=(lay["j_plain"], lay["n_mix"]), **proj_kw)
    h0 = _ctx_states(pc, dtc, alog_row, conv_loc, bsz=bsz, q=ctx_len)

    p, dt = _inproj(x1, mod3, lat_row(tm_proj), *proj_args, seg=GRID_W, **proj_kw)
    us = _gmlp(p, a_width, gmlp_ln_w[i], gmlp_ln_b[i], bf(gmlp_ws[i]), gmlp_bs[i].T, tm=tm)
    yf, yb = _ssd(p, dt, alog_row, dskip_row, h0, conv_loc, bsz=bsz, nc=seq // SSD_CHUNK,
                  p_blk=lay["mix0"] // lay["mix_width"], p_width=lay["mix_width"])
    yn = _gnorm(yf, yb, p, col_z // inner, ssm_norm[i], tm=256)
    mg = _merge(us, yn, p, lay, b_gate[i].reshape(1, -1), bf(w_a[i]), bf(w_b[i]), tm=1024)
    x3 = _outproj(mg, x1, mod3, lat_row(tm), bf(w_out[i]), tm=tm)

    out = _ffn(x3, mod3, 6, lat_row(tm), norm_ffn2[i], bf(ffn2_gate[i]), bf(ffn2_up[i]), bf(ffn2_down[i]),
               norm_final, tm=tm, tf=512)
    return out.reshape(bsz, seq, d)
```

```python
import functools

import jax
import jax.numpy as jnp
from jax import lax
from jax.experimental import pallas as pl
from jax.experimental.pallas import tpu as pltpu

F32 = jnp.float32
BF16 = jnp.bfloat16
EPS = 1e-6
LOG2E = 1.4426950408889634

GRID_W = 64
A_GROUPS = 8
A_CHUNK = 128
HEADDIM = 64
GROUPS = 8
STATE = 128
CONV_K = 5
SSD_CHUNK = 128
N_MOD = 9

LANES = 128
SUB = 256
OUT_SUB = 512
PROJ_SUB = 512
PROJ_CHUNKS = (512, 512, 512, 256, 256)
SUBLANES = 8
BF16_ROWS = 16
VMEM_LIMIT = 56 * 1024 * 1024


def _cparams(sem):
    return pltpu.CompilerParams(dimension_semantics=sem, vmem_limit_bytes=VMEM_LIMIT)


def _dot(a, b):
    return jnp.dot(a, b, preferred_element_type=F32)


def _split3(a):
    a1 = a.astype(BF16)
    r1 = a - a1.astype(F32)
    a2 = r1.astype(BF16)
    r2 = r1 - a2.astype(F32)
    return a1, a2, r2.astype(BF16)


def _dot_exact_lhs(tri, a):
    a1, a2, a3 = _split3(a)
    return _dot(tri, a1) + _dot(tri, a2) + _dot(tri, a3)


def _dot_exact_rhs(a, tri):
    a1, a2, a3 = _split3(a)
    return _dot(a1, tri) + _dot(a2, tri) + _dot(a3, tri)


def _rms_mod_store(x_ref, h_ref, nw, scale, shift):
    gain = nw * (1.0 + scale)
    for r in range(0, x_ref.shape[0], BF16_ROWS):
        x = x_ref[r:r + BF16_ROWS, :]
        ms = jnp.mean(x * x, axis=-1, keepdims=True)
        h_ref[r:r + BF16_ROWS, :] = (x * lax.rsqrt(ms + EPS) * gain + shift).astype(BF16)


def _adaln_kernel(c_ref, w_ref, b_ref, o_ref):
    s = jax.nn.silu(c_ref[...])
    s1, s2, s3 = _split3(s)
    w = w_ref[...].astype(BF16)
    o_ref[...] = _dot(s1, w) + _dot(s2, w) + _dot(s3, w) + b_ref[...]


def _adaln(c8, w_mod, b_mod):
    d, n = w_mod.shape
    tn = 1024
    return pl.pallas_call(
        _adaln_kernel,
        grid=(n // tn,),
        in_specs=[pl.BlockSpec((8, d), lambda j: (0, 0)),
                  pl.BlockSpec((d, tn), lambda j: (0, j)),
                  pl.BlockSpec((1, tn), lambda j: (0, j))],
        out_specs=pl.BlockSpec((8, tn), lambda j: (0, j)),
        out_shape=jax.ShapeDtypeStruct((8, n), F32),
        compiler_params=_cparams(("arbitrary",)),
        name="adaln",
    )(c8, w_mod, b_mod.reshape(1, n))


def _ffn_kernel(x_ref, sh_ref, sc_ref, gt_ref, nw_ref, wg_ref, wu_ref, wd_ref, fw_ref, o_ref,
                h_scr, acc_scr, *, final_norm):
    j = pl.program_id(1)

    @pl.when(j == 0)
    def _():
        _rms_mod_store(x_ref, h_scr, nw_ref[...], sc_ref[...], sh_ref[...])
        acc_scr[...] = jnp.zeros_like(acc_scr)

    acts = []
    for s in range(wg_ref.shape[1] // SUB):
        cols = slice(s * SUB, (s + 1) * SUB)
        g = _dot(h_scr[...], wg_ref[:, cols])
        u = _dot(h_scr[...], wu_ref[:, cols])
        acts.append((jax.nn.silu(g) * u).astype(BF16))
    acc_scr[...] += _dot(jnp.concatenate(acts, axis=1), wd_ref[...])

    @pl.when(j == pl.num_programs(1) - 1)
    def _():
        half_gate = 0.5 * gt_ref[...]
        for r in range(0, x_ref.shape[0], SUBLANES):
            rows = slice(r, r + SUBLANES)
            out = x_ref[rows, :] + half_gate * acc_scr[rows, :]
            if final_norm:
                ms = jnp.mean(out * out, axis=-1, keepdims=True)
                out = out * lax.rsqrt(ms + EPS) * fw_ref[...]
            o_ref[rows, :] = out


def _ffn(x2, mod3, mod_k, mod_row, norm_w, wg, wu, wd, final_w, *, tm, tf):
    t, d = x2.shape
    f = wg.shape[1]
    final_norm = final_w is not None
    fw = final_w if final_norm else norm_w
    assert t % tm == 0 and f % tf == 0

    def mod_spec(k):
        return pl.BlockSpec((None, 1, d), lambda i, j: (mod_row(i), 0, k))

    return pl.pallas_call(
        functools.partial(_ffn_kernel, final_norm=final_norm),
        grid=(t // tm, f // tf),
        in_specs=[pl.BlockSpec((tm, d), lambda i, j: (i, 0)),
                  mod_spec(mod_k), mod_spec(mod_k + 1), mod_spec(mod_k + 2),
                  pl.BlockSpec((1, d), lambda i, j: (0, 0)),
                  pl.BlockSpec((d, tf), lambda i, j: (0, j)),
                  pl.BlockSpec((d, tf), lambda i, j: (0, j)),
                  pl.BlockSpec((tf, d), lambda i, j: (j, 0)),
                  pl.BlockSpec((1, d), lambda i, j: (0, 0))],
        out_specs=pl.BlockSpec((tm, d), lambda i, j: (i, 0)),
        out_shape=jax.ShapeDtypeStruct((t, d), F32),
        scratch_shapes=[pltpu.VMEM((tm, d), BF16), pltpu.VMEM((tm, d), F32)],
        compiler_params=_cparams(("parallel", "arbitrary")),
        name="ffn",
    )(x2, mod3, mod3, mod3, norm_w.reshape(1, d), wg, wu, wd, fw.reshape(1, d))


def _shift_in_segment(x4, off):
    nseg, _, sub, c = x4.shape
    r = pltpu.roll(x4, (-off) % sub, axis=2)
    zero = jnp.zeros((nseg, 1, sub, c), x4.dtype)
    row = lax.broadcasted_iota(jnp.int32, (1, 1, sub, c), 2)
    if off < 0:
        return jnp.where(row < -off, jnp.concatenate([zero, r[:, :-1]], axis=1), r)
    return jnp.where(row >= sub - off, jnp.concatenate([r[:, 1:], zero], axis=1), r)


def _inproj_kernel(x_ref, sh_ref, sc_ref, nw_ref, w_ref, wdt_ref, dtb_ref, cw_ref, cb_ref,
                   p_ref, dt_ref, h_scr, acc_scr, *, seg, j0, j_gelu, j_plain, j_conv):
    j = pl.program_id(1) + j0
    tm, tn = p_ref.shape

    @pl.when(j == j0)
    def _():
        _rms_mod_store(x_ref, h_scr, nw_ref[...], sc_ref[...], sh_ref[...])
        dt_ref[...] = jax.nn.softplus(_dot(h_scr[...], wdt_ref[...]) + dtb_ref[...])

    def run(epilogue):
        assert sum(PROJ_CHUNKS) == tn
        starts = [sum(PROJ_CHUNKS[:k]) for k in range(len(PROJ_CHUNKS))]
        cols = [slice(a, a + w) for a, w in zip(starts, PROJ_CHUNKS)]

        def finish(s):
            c = cols[s]
            p_ref[:, c] = epilogue(acc_scr[s % 2, :, :PROJ_CHUNKS[s]], c).astype(BF16)

        for s, c in enumerate(cols):
            if s > 0:
                finish(s - 1)
            acc_scr[s % 2, :, :PROJ_CHUNKS[s]] = _dot(h_scr[...], w_ref[:, c])
        finish(len(cols) - 1)

    @pl.when(j < j_gelu)
    def _():
        run(lambda acc, cols: jax.nn.gelu(acc))

    @pl.when(((j >= j_gelu) & (j < j_plain)) | (j >= j_conv))
    def _():
        run(lambda acc, cols: acc)

    @pl.when((j >= j_plain) & (j < j_conv))
    def _():
        pad = (CONV_K - 1) // 2

        def conv_silu(acc, cols):
            x4 = acc.reshape(tm // seg, seg // SUBLANES, SUBLANES, acc.shape[1])
            y = cb_ref[:, cols] + cw_ref[pad:pad + 1, cols] * x4
            for tap in range(CONV_K):
                if tap != pad:
                    y = y + cw_ref[tap:tap + 1, cols] * _shift_in_segment(x4, tap - pad)
            return jax.nn.silu(y).reshape(acc.shape)

        run(conv_silu)


def _inproj(x2, mod3, mod_row, norm_w, w_all, w_dt, dt_bias, conv_w, conv_b, *, seg, tm, tn,
            n_gelu, n_plain, n_conv, n_gate, tiles=None):
    t, d = x2.shape
    j_gelu = n_gelu // tn
    j_plain = j_gelu + n_plain // tn
    j_conv = j_plain + n_conv // tn
    nconv_t = n_conv // tn
    ngate_t = n_gate // tn
    n_dt = w_dt.shape[1]
    j0, nj = tiles if tiles is not None else (0, j_conv + ngate_t)
    assert t % tm == 0 and tm % seg == 0 and w_all.shape[1] == (j_conv + ngate_t) * tn + n_dt

    def mod_spec(k):
        return pl.BlockSpec((None, 1, d), lambda i, j: (mod_row(i), 0, k))

    def w_col(j):
        return pl.multiple_of(jnp.where(j < j_conv, j * tn, j * tn + n_dt), LANES)

    return pl.pallas_call(
        functools.partial(_inproj_kernel, seg=seg, j0=j0, j_gelu=j_gelu, j_plain=j_plain, j_conv=j_conv),
        grid=(t // tm, nj),
        in_specs=[pl.BlockSpec((tm, d), lambda i, j: (i, 0)),
                  mod_spec(3), mod_spec(4),
                  pl.BlockSpec((1, d), lambda i, j: (0, 0)),
                  pl.BlockSpec((pl.Element(d), pl.Element(tn)), lambda i, j: (0, w_col(j + j0))),
                  pl.BlockSpec((d, LANES), lambda i, j: (0, 0)),
                  pl.BlockSpec((1, LANES), lambda i, j: (0, 0)),
                  pl.BlockSpec((CONV_K, tn), lambda i, j: (0, jnp.clip(j + j0 - j_plain, 0, nconv_t - 1))),
                  pl.BlockSpec((1, tn), lambda i, j: (0, jnp.clip(j + j0 - j_plain, 0, nconv_t - 1)))],
        out_specs=[pl.BlockSpec((tm, tn), lambda i, j: (i, j)),
                   pl.BlockSpec((tm, LANES), lambda i, j: (i, 0))],
        out_shape=[jax.ShapeDtypeStruct((t, nj * tn), BF16), jax.ShapeDtypeStruct((t, LANES), F32)],
        scratch_shapes=[pltpu.VMEM((tm, d), BF16), pltpu.VMEM((2, tm, PROJ_SUB), F32)],
        compiler_params=_cparams(("parallel", "arbitrary")),
        name="inproj",
    )(x2, mod3, mod3, norm_w.reshape(1, d), w_all, w_dt, dt_bias, conv_w, conv_b)


def _gmlp_kernel(u_ref, v_ref, lw_ref, lb_ref, ws_ref, bs_ref, o_ref):
    v = v_ref[...].astype(F32)
    mu = jnp.mean(v, axis=-1, keepdims=True)
    vc = v - mu
    var = jnp.mean(vc * vc, axis=-1, keepdims=True)
    vn = (vc * lax.rsqrt(var + EPS) * lw_ref[...] + lb_ref[...]).astype(BF16)
    tm, width = vn.shape
    gw = width // A_GROUPS
    for c in range(tm // A_CHUNK):
        r0 = c * A_CHUNK
        for g in range(A_GROUPS):
            c0 = g * gw
            s = _dot(ws_ref[g], vn[r0:r0 + A_CHUNK, c0:c0 + gw]) + bs_ref[:, g:g + 1]
            u = u_ref[r0:r0 + A_CHUNK, c0:c0 + gw].astype(F32)
            o_ref[r0:r0 + A_CHUNK, c0:c0 + gw] = (u * s).astype(BF16)


def _gmlp(p, width, ln_w, ln_b, ws, bs_t, *, tm):
    t = p.shape[0]
    return pl.pallas_call(
        _gmlp_kernel,
        grid=(t // tm,),
        in_specs=[pl.BlockSpec((tm, width), lambda i: (i, 0)),
                  pl.BlockSpec((tm, width), lambda i: (i, 1)),
                  pl.BlockSpec((1, width), lambda i: (0, 0)),
                  pl.BlockSpec((1, width), lambda i: (0, 0)),
                  pl.BlockSpec((A_GROUPS, A_CHUNK, A_CHUNK), lambda i: (0, 0, 0)),
                  pl.BlockSpec((A_CHUNK, A_GROUPS), lambda i: (0, 0))],
        out_specs=pl.BlockSpec((tm, width), lambda i: (i, 0)),
        out_shape=jax.ShapeDtypeStruct((t, width), BF16),
        compiler_params=_cparams(("parallel",)),
        name="gmlp",
    )(p, p, ln_w.reshape(1, width), ln_b.reshape(1, width), ws, bs_t)


HPG = 8
GL = 2 * HPG


def _decay_tables(dt, alog_row):
    q = dt.shape[0]
    a = dt * (-jnp.exp(alog_row) * LOG2E)
    ri = lax.broadcasted_iota(jnp.int32, (q, q), 0)
    ci = lax.broadcasted_iota(jnp.int32, (q, q), 1)
    tril = jnp.where(ri >= ci, 1.0, 0.0).astype(BF16)
    triu = jnp.where(ri <= ci, 1.0, 0.0).astype(BF16)
    lane_f = (lax.broadcasted_iota(jnp.int32, (q, LANES), 1) % GL) < HPG
    sub_f = (lax.broadcasted_iota(jnp.int32, (LANES, q), 0) % GL) < HPG
    cs = jnp.where(lane_f, _dot_exact_lhs(tril, a), _dot_exact_lhs(triu, a))
    a_t = a.T
    cst = jnp.where(sub_f, _dot_exact_rhs(a_t, triu), _dot_exact_rhs(a_t, tril))
    return cs, cst - jnp.log2(dt.T)


def _pair_masks(q):
    lane = lax.broadcasted_iota(jnp.int32, (q, LANES), 1)
    return (jnp.where(lane < HEADDIM, 1.0, 0.0).astype(BF16),
            jnp.where(lane >= HEADDIM, 1.0, 0.0).astype(BF16))


def _block_diag_pair(xp, masks):
    return jnp.concatenate([xp * masks[0], xp * masks[1]], axis=0)


def _head_terms(cs_ref, rowt_ref, hl, last):
    q = cs_ref.shape[0]
    col = cs_ref[:, hl:hl + 1]
    rowt = rowt_ref[hl:hl + 1, :]
    clast = jnp.broadcast_to(col[last:last + 1, :], (1, q))
    return col, rowt, clast


def _ctx_state_kernel(x_ref, b_ref, dt_ref, alog_ref, h_ref, cs_scr, rowt_scr):
    q = x_ref.shape[0]
    cs, rowt = _decay_tables(dt_ref[...], alog_ref[...])
    for g in range(GROUPS):
        cs_scr[g] = cs[:, g * GL:(g + 1) * GL]
        rowt_scr[g] = rowt[g * GL:(g + 1) * GL, :]
    masks = _pair_masks(q)

    def group_body(g, carry):
        cs_g, rowt_g = cs_scr[g], rowt_scr[g]
        bg = b_ref[:, pl.ds(pl.multiple_of(g * STATE, STATE), STATE)]
        bg_t = bg.astype(F32).T.astype(BF16)
        for d in range(2):
            last = q - 1 if d == 0 else 0
            for pair in range(HPG // 2):
                col0 = pl.multiple_of(g * (HPG * HEADDIM) + pair * LANES, LANES)
                bd = _block_diag_pair(x_ref[:, pl.ds(col0, LANES)], masks)
                bts = []
                for k in range(2):
                    _, rowt, clast = _head_terms(cs_g, rowt_g, d * HPG + pair * 2 + k, last)
                    bts.append(bg_t * jnp.exp2(clast - rowt).astype(BF16))
                h_ref[d, g, :, pair * LANES:(pair + 1) * LANES] = _dot(jnp.concatenate(bts, axis=1), bd)
        return carry

    lax.fori_loop(0, GROUPS, group_body, 0)


def _ctx_states(pc, dtc, alog_row, *, bsz, q, x_blk, b_blk):
    inner = GROUPS * HPG * HEADDIM
    return pl.pallas_call(
        _ctx_state_kernel,
        grid=(bsz,),
        in_specs=[pl.BlockSpec((q, inner), lambda b: (b, x_blk)),
                  pl.BlockSpec((q, GROUPS * STATE), lambda b: (b, b_blk)),
                  pl.BlockSpec((q, LANES), lambda b: (b, 0)),
                  pl.BlockSpec((1, LANES), lambda b: (0, 0))],
        out_specs=pl.BlockSpec((2, None, GROUPS, STATE, HPG * HEADDIM), lambda b: (0, b, 0, 0, 0)),
        out_shape=jax.ShapeDtypeStruct((2, bsz, GROUPS, STATE, HPG * HEADDIM), F32),
        scratch_shapes=[pltpu.VMEM((GROUPS, q, GL), F32), pltpu.VMEM((GROUPS, GL, q), F32)],
        compiler_params=_cparams(("arbitrary",)),
        name="ctx_state",
    )(pc, pc, dtc, alog_row)


def _ssd_kernel(xf_ref, bf_ref, cf_ref, dtf_ref, xb_ref, bb_ref, cb_ref, dtb_ref, alog_ref, dskip_ref, h0_ref,
                yf_ref, yb_ref, s_scr, cs_scr, rowt_scr):
    q = xf_ref.shape[0]

    @pl.when(pl.program_id(1) == 0)
    def _():
        s_scr[...] = h0_ref[...]

    lane_f = (lax.broadcasted_iota(jnp.int32, (q, LANES), 1) % GL) < HPG
    dt = jnp.where(lane_f, dtf_ref[...], dtb_ref[...])
    cs_scr[...], rowt_scr[...] = _decay_tables(dt, alog_ref[...])

    ri = lax.broadcasted_iota(jnp.int32, (q, q), 0)
    ci = lax.broadcasted_iota(jnp.int32, (q, q), 1)
    lane_lo = lax.broadcasted_iota(jnp.int32, (q, LANES), 1) < HEADDIM
    lane_lo_row = lax.broadcasted_iota(jnp.int32, (1, LANES), 1) < HEADDIM
    masks = _pair_masks(q)
    dirs = ((xf_ref, bf_ref, cf_ref, yf_ref), (xb_ref, bb_ref, cb_ref, yb_ref))

    def group_body(g):
        for d, (x_ref, b_ref, c_ref, y_ref) in enumerate(dirs):
            last = q - 1 if d == 0 else 0
            mask = (ri >= ci) if d == 0 else (ri <= ci)
            gcol = slice(g * STATE, (g + 1) * STATE)
            bg = b_ref[:, gcol]
            cg = c_ref[:, gcol]
            cb = lax.dot_general(cg, bg, (((1,), (1,)), ((), ())), preferred_element_type=F32)
            bg_t = bg.astype(F32).T.astype(BF16)
            for pair in range(HPG // 2):
                psl = slice(pair * LANES, (pair + 1) * LANES)
                s_pair = s_scr[d, g, :, psl]
                y_off = _dot(cg, s_pair.astype(BF16))
                col0 = g * (HPG * HEADDIM) + pair * LANES
                xp = x_ref[:, pl.ds(col0, LANES)]
                bd = _block_diag_pair(xp, masks)
                ms, bts, ecols, elasts = [], [], [], []
                for k in range(2):
                    hl = g * GL + d * HPG + pair * 2 + k
                    col, rowt, clast = _head_terms(cs_scr, rowt_scr, hl, last)
                    colb = jnp.broadcast_to(col, (q, q))
                    ms.append((cb * jnp.exp2(jnp.where(mask, colb - rowt, -jnp.inf))).astype(BF16))
                    bts.append(bg_t * jnp.exp2(clast - rowt).astype(BF16))
                    ecols.append(jnp.exp2(colb))
                    elasts.append(jnp.exp2(clast))
                y = _dot(jnp.concatenate(ms, axis=1), bd)
                y = y + y_off * jnp.where(lane_lo, ecols[0], ecols[1])
                if d == 0:
                    y = y + xp.astype(F32) * dskip_ref[:, pl.ds(col0, LANES)]
                y_ref[:, pl.ds(col0, LANES)] = y.astype(BF16)
                grow = jnp.where(lane_lo_row, elasts[0], elasts[1])
                s_scr[d, g, :, psl] = s_pair * grow + _dot(jnp.concatenate(bts, axis=1), bd)

    for g in range(GROUPS):
        group_body(g)


def _ssd(p, dt, alog_row, dskip_row, h0, *, bsz, nc, x_blk, b_blk, c_blk):
    q = SSD_CHUNK
    t = p.shape[0]
    inner = GROUPS * HPG * HEADDIM
    gs = GROUPS * STATE

    def fwd(b, c):
        return b * nc + c

    def bwd(b, c):
        return b * nc + (nc - 1 - c)

    def specs(row):
        return [pl.BlockSpec((q, inner), lambda b, c: (row(b, c), x_blk)),
                pl.BlockSpec((q, gs), lambda b, c: (row(b, c), b_blk)),
                pl.BlockSpec((q, gs), lambda b, c: (row(b, c), c_blk)),
                pl.BlockSpec((q, LANES), lambda b, c: (row(b, c), 0))]

    return pl.pallas_call(
        _ssd_kernel,
        grid=(bsz, nc),
        in_specs=specs(fwd) + specs(bwd) + [
            pl.BlockSpec((1, LANES), lambda b, c: (0, 0)),
            pl.BlockSpec((1, inner), lambda b, c: (0, 0)),
            pl.BlockSpec((2, None, GROUPS, STATE, HPG * HEADDIM), lambda b, c: (0, b, 0, 0, 0))],
        out_specs=[pl.BlockSpec((q, inner), lambda b, c: (fwd(b, c), 0)),
                   pl.BlockSpec((q, inner), lambda b, c: (bwd(b, c), 0))],
        out_shape=[jax.ShapeDtypeStruct((t, inner), BF16), jax.ShapeDtypeStruct((t, inner), BF16)],
        scratch_shapes=[pltpu.VMEM((2, GROUPS, STATE, HPG * HEADDIM), F32),
                        pltpu.VMEM((q, LANES), F32), pltpu.VMEM((LANES, q), F32)],
        compiler_params=_cparams(("arbitrary", "arbitrary")),
        name="ssd",
    )(p, p, p, dt, p, p, p, dt, alog_row, dskip_row, h0)


def _gnorm_kernel(yf_ref, yb_ref, z_ref, w_ref, o_ref):
    y = (yf_ref[...].astype(F32) + yb_ref[...].astype(F32)) * jax.nn.silu(z_ref[...].astype(F32))
    gw = y.shape[1] // GROUPS
    for g in range(GROUPS):
        yg = y[:, g * gw:(g + 1) * gw]
        ms = jnp.mean(yg * yg, axis=-1, keepdims=True)
        o_ref[:, g * gw:(g + 1) * gw] = (yg * lax.rsqrt(ms + EPS) * w_ref[:, g * gw:(g + 1) * gw]).astype(BF16)


def _gnorm(yf, yb, p, z_blk, norm_w, *, tm):
    t, inner = yf.shape
    return pl.pallas_call(
        _gnorm_kernel,
        grid=(t // tm,),
        in_specs=[pl.BlockSpec((tm, inner), lambda i: (i, 0)),
                  pl.BlockSpec((tm, inner), lambda i: (i, 0)),
                  pl.BlockSpec((tm, inner), lambda i: (i, z_blk)),
                  pl.BlockSpec((1, inner), lambda i: (0, 0))],
        out_specs=pl.BlockSpec((tm, inner), lambda i: (i, 0)),
        out_shape=jax.ShapeDtypeStruct((t, inner), BF16),
        compiler_params=_cparams(("parallel",)),
        name="gnorm",
    )(yf, yb, p, norm_w.reshape(1, inner))


def _merge_kernel(us_ref, yn_ref, ga_ref, gb_ref, ba_ref, bb_ref, wa_ref, wb_ref, o_ref):
    for s in range(o_ref.shape[1] // SUB):
        cols = slice(s * SUB, (s + 1) * SUB)
        ya = _dot(us_ref[...], wa_ref[:, cols])
        yb = _dot(yn_ref[...], wb_ref[:, cols])
        ga = jax.nn.sigmoid(ga_ref[:, cols].astype(F32) + ba_ref[:, cols])
        gb = jax.nn.sigmoid(gb_ref[:, cols].astype(F32) + bb_ref[:, cols])
        o_ref[:, cols] = (ga * ya + gb * yb).astype(BF16)


def _merge(us, yn, p, gate_col0, b_gate, wa, wb, *, tm, tn):
    t = us.shape[0]
    wa_in, d = wa.shape
    wb_in = wb.shape[0]
    ga_blk = gate_col0 // tn
    gb_blk = (gate_col0 + d) // tn
    return pl.pallas_call(
        _merge_kernel,
        grid=(t // tm, d // tn),
        in_specs=[pl.BlockSpec((tm, wa_in), lambda i, j: (i, 0)),
                  pl.BlockSpec((tm, wb_in), lambda i, j: (i, 0)),
                  pl.BlockSpec((tm, tn), lambda i, j: (i, ga_blk + j)),
                  pl.BlockSpec((tm, tn), lambda i, j: (i, gb_blk + j)),
                  pl.BlockSpec((1, tn), lambda i, j: (0, j)),
                  pl.BlockSpec((1, tn), lambda i, j: (0, d // tn + j)),
                  pl.BlockSpec((wa_in, tn), lambda i, j: (0, j)),
                  pl.BlockSpec((wb_in, tn), lambda i, j: (0, j))],
        out_specs=pl.BlockSpec((tm, tn), lambda i, j: (i, j)),
        out_shape=jax.ShapeDtypeStruct((t, d), BF16),
        compiler_params=_cparams(("parallel", "arbitrary")),
        name="merge",
    )(us, yn, p, p, b_gate, b_gate, wa, wb)


def _outproj_kernel(mg_ref, x_ref, m_ref, wo_ref, o_ref):
    for s in range(o_ref.shape[1] // OUT_SUB):
        cols = slice(s * OUT_SUB, (s + 1) * OUT_SUB)
        o_ref[:, cols] = x_ref[:, cols] + m_ref[:, cols] * _dot(mg_ref[...], wo_ref[:, cols])


def _outproj(mg, x2, mod3, mod_row, wo, *, tm):
    t, d = x2.shape
    return pl.pallas_call(
        _outproj_kernel,
        grid=(t // tm,),
        in_specs=[pl.BlockSpec((tm, d), lambda i: (i, 0)),
                  pl.BlockSpec((tm, d), lambda i: (i, 0)),
                  pl.BlockSpec((None, 1, d), lambda i: (mod_row(i), 0, 5)),
                  pl.BlockSpec((d, d), lambda i: (0, 0))],
        out_specs=pl.BlockSpec((tm, d), lambda i: (i, 0)),
        out_shape=jax.ShapeDtypeStruct((t, d), F32),
        compiler_params=_cparams(("parallel",)),
        name="outproj",
    )(mg, x2, mod3, wo)


def _dt_perm(a):
    lead = a.shape[:-2]
    a = a.reshape(*lead, 2, GROUPS, HPG)
    a = jnp.swapaxes(a, -3, -2)
    return a.reshape(*lead, 2 * GROUPS * HPG)


def kernel(x, c, ctx, c_ctx, w_mod, b_mod, norm_ffn1, ffn1_gate, ffn1_up, ffn1_down, norm_mix, w_in, b_gate,
           gmlp_ln_w, gmlp_ln_b, gmlp_ws, gmlp_bs, w_a, conv_w, conv_b, a_log, dt_bias, d_skip, ssm_norm,
           w_b, w_out, norm_ffn2, ffn2_gate, ffn2_up, ffn2_down, norm_final):
    bsz, seq, d = x.shape
    ctx_len = ctx.shape[1]
    depth = w_mod.shape[0]
    assert depth == 1, "context stream update between layers is not implemented"
    assert bsz < 8
    inner = GROUPS * HPG * HEADDIM
    gs = GROUPS * STATE
    a_width = w_a.shape[1]
    n_heads = GROUPS * HPG
    assert w_b.shape[1] == inner and conv_w.shape[2] == inner + 2 * gs
    assert w_in.shape[2] == 2 * a_width + 2 * inner + 2 * gs + 2 * n_heads + 2 * d

    t = bsz * seq
    tc = bsz * ctx_len
    x2 = x.reshape(t, d)
    ctx2 = ctx.reshape(tc, d)

    off_dt = 2 * a_width + 2 * inner + 2 * gs
    off_gate = off_dt + 2 * n_heads
    n_gelu, n_plain, n_conv = 2 * a_width, inner, inner + 2 * gs
    col_z = n_gelu
    col_x = n_gelu + n_plain
    col_b = col_x + inner
    col_c = col_b + gs
    col_gate = n_gelu + n_plain + n_conv

    i = 0
    w_all = w_in[i].astype(BF16)
    w_dt = _dt_perm(w_in[i][:, off_dt:off_gate].reshape(d, 2, n_heads)).astype(BF16)
    dt_bias_row = _dt_perm(dt_bias[i]).reshape(1, 2 * n_heads)
    alog_row = _dt_perm(a_log[i].astype(F32)).reshape(1, 2 * n_heads)
    dskip_row = jnp.repeat(d_skip[i], HEADDIM).reshape(1, inner)
    bf = lambda w: w.astype(BF16)

    c8 = jnp.zeros((8, d), F32).at[:bsz].set(c).at[bsz].set(c_ctx)
    mod3 = _adaln(c8, w_mod[i], b_mod[i]).reshape(8, 1, N_MOD * d)

    tm = 512
    tm_proj = 512

    def lat_row(rows):
        assert seq % rows == 0, "a token tile must not straddle two samples"
        return lambda ti: (ti * rows) // seq

    ctx_row = lambda ti: bsz

    ffn1_w = (bf(ffn1_gate[i]), bf(ffn1_up[i]), bf(ffn1_down[i]))
    x1 = _ffn(x2, mod3, 0, lat_row(tm), norm_ffn1[i], *ffn1_w, None, tm=tm, tf=512)
    ctx1 = _ffn(ctx2, mod3, 0, ctx_row, norm_ffn1[i], *ffn1_w, None, tm=tm, tf=512)

    tn_proj = 2048
    proj_args = (norm_mix[i], w_all, w_dt, dt_bias_row, conv_w[i], conv_b[i].reshape(1, -1))
    proj_kw = dict(tm=tm_proj, tn=tn_proj, n_gelu=n_gelu, n_plain=n_plain, n_conv=n_conv, n_gate=2 * d)
    ctx_tiles = (col_x // tn_proj, n_conv // tn_proj)
    pc, dtc = _inproj(ctx1, mod3, ctx_row, *proj_args, seg=ctx_len, tiles=ctx_tiles, **proj_kw)
    h0 = _ctx_states(pc, dtc, alog_row, bsz=bsz, q=ctx_len, x_blk=0, b_blk=inner // gs)

    p, dt = _inproj(x1, mod3, lat_row(tm_proj), *proj_args, seg=GRID_W, **proj_kw)
    us = _gmlp(p, a_width, gmlp_ln_w[i], gmlp_ln_b[i], bf(gmlp_ws[i]), gmlp_bs[i].T, tm=tm)
    yf, yb = _ssd(p, dt, alog_row, dskip_row, h0, bsz=bsz, nc=seq // SSD_CHUNK,
                  x_blk=col_x // inner, b_blk=col_b // gs, c_blk=col_c // gs)
    yn = _gnorm(yf, yb, p, col_z // inner, ssm_norm[i], tm=256)
    mg = _merge(us, yn, p, col_gate, b_gate[i].reshape(1, -1), bf(w_a[i]), bf(w_b[i]), tm=1024, tn=512)
    x3 = _outproj(mg, x1, mod3, lat_row(tm), bf(w_out[i]), tm=tm)

    out = _ffn(x3, mod3, 6, lat_row(tm), norm_ffn2[i], bf(ffn2_gate[i]), bf(ffn2_up[i]), bf(ffn2_down[i]),
               norm_final, tm=tm, tf=512)
    return out.reshape(bsz, seq, d)
```

```python
import functools

import jax
import jax.numpy as jnp
from jax import lax
from jax.experimental import pallas as pl
from jax.experimental.pallas import tpu as pltpu

F32 = jnp.float32
BF16 = jnp.bfloat16
EPS = 1e-6
LOG2E = 1.4426950408889634

GRID_W = 64
A_GROUPS = 8
A_CHUNK = 128
HEADDIM = 64
GROUPS = 8
STATE = 128
CONV_K = 5
SSD_CHUNK = 128
N_MOD = 9

LANES = 128
SUB = 256
OUT_SUB = 512
PROJ_SUB = 512
PROJ_CHUNKS = (512, 256, 256, 512, 256, 256)
SUBLANES = 8
BF16_ROWS = 16
VMEM_LIMIT = 56 * 1024 * 1024


def _cparams(sem):
    return pltpu.CompilerParams(dimension_semantics=sem, vmem_limit_bytes=VMEM_LIMIT)


def _dot(a, b):
    return jnp.dot(a, b, preferred_element_type=F32)


def _split3(a):
    a1 = a.astype(BF16)
    r1 = a - a1.astype(F32)
    a2 = r1.astype(BF16)
    r2 = r1 - a2.astype(F32)
    return a1, a2, r2.astype(BF16)


def _dot_exact_lhs(tri, a):
    a1, a2, a3 = _split3(a)
    return _dot(tri, a1) + _dot(tri, a2) + _dot(tri, a3)


def _dot_exact_rhs(a, tri):
    a1, a2, a3 = _split3(a)
    return _dot(a1, tri) + _dot(a2, tri) + _dot(a3, tri)


def _rms_mod_store(x_ref, h_ref, nw, scale, shift):
    gain = nw * (1.0 + scale)
    for r in range(0, x_ref.shape[0], BF16_ROWS):
        x = x_ref[r:r + BF16_ROWS, :]
        ms = jnp.mean(x * x, axis=-1, keepdims=True)
        h_ref[r:r + BF16_ROWS, :] = (x * lax.rsqrt(ms + EPS) * gain + shift).astype(BF16)


def _adaln_kernel(c_ref, w_ref, b_ref, o_ref):
    s = jax.nn.silu(c_ref[...])
    s1, s2, s3 = _split3(s)
    w = w_ref[...].astype(BF16)
    o_ref[...] = _dot(s1, w) + _dot(s2, w) + _dot(s3, w) + b_ref[...]


def _adaln(c8, w_mod, b_mod):
    d, n = w_mod.shape
    tn = 1024
    return pl.pallas_call(
        _adaln_kernel,
        grid=(n // tn,),
        in_specs=[pl.BlockSpec((8, d), lambda j: (0, 0)),
                  pl.BlockSpec((d, tn), lambda j: (0, j)),
                  pl.BlockSpec((1, tn), lambda j: (0, j))],
        out_specs=pl.BlockSpec((8, tn), lambda j: (0, j)),
        out_shape=jax.ShapeDtypeStruct((8, n), F32),
        compiler_params=_cparams(("arbitrary",)),
        name="adaln",
    )(c8, w_mod, b_mod.reshape(1, n))


def _ffn_kernel(x_ref, sh_ref, sc_ref, gt_ref, nw_ref, wg_ref, wu_ref, wd_ref, fw_ref, o_ref,
                h_scr, acc_scr, *, final_norm):
    j = pl.program_id(1)

    @pl.when(j == 0)
    def _():
        _rms_mod_store(x_ref, h_scr, nw_ref[...], sc_ref[...], sh_ref[...])
        acc_scr[...] = jnp.zeros_like(acc_scr)

    acts = []
    for s in range(wg_ref.shape[1] // SUB):
        cols = slice(s * SUB, (s + 1) * SUB)
        g = _dot(h_scr[...], wg_ref[:, cols])
        u = _dot(h_scr[...], wu_ref[:, cols])
        acts.append((jax.nn.silu(g) * u).astype(BF16))
    acc_scr[...] += _dot(jnp.concatenate(acts, axis=1), wd_ref[...])

    @pl.when(j == pl.num_programs(1) - 1)
    def _():
        half_gate = 0.5 * gt_ref[...]
        for r in range(0, x_ref.shape[0], SUBLANES):
            rows = slice(r, r + SUBLANES)
            out = x_ref[rows, :] + half_gate * acc_scr[rows, :]
            if final_norm:
                ms = jnp.mean(out * out, axis=-1, keepdims=True)
                out = out * lax.rsqrt(ms + EPS) * fw_ref[...]
            o_ref[rows, :] = out


def _ffn(x2, mod3, mod_k, mod_row, norm_w, wg, wu, wd, final_w, *, tm, tf):
    t, d = x2.shape
    f = wg.shape[1]
    final_norm = final_w is not None
    fw = final_w if final_norm else norm_w
    assert t % tm == 0 and f % tf == 0

    def mod_spec(k):
        return pl.BlockSpec((None, 1, d), lambda i, j: (mod_row(i), 0, k))

    return pl.pallas_call(
        functools.partial(_ffn_kernel, final_norm=final_norm),
        grid=(t // tm, f // tf),
        in_specs=[pl.BlockSpec((tm, d), lambda i, j: (i, 0)),
                  mod_spec(mod_k), mod_spec(mod_k + 1), mod_spec(mod_k + 2),
                  pl.BlockSpec((1, d), lambda i, j: (0, 0)),
                  pl.BlockSpec((d, tf), lambda i, j: (0, j)),
                  pl.BlockSpec((d, tf), lambda i, j: (0, j)),
                  pl.BlockSpec((tf, d), lambda i, j: (j, 0)),
                  pl.BlockSpec((1, d), lambda i, j: (0, 0))],
        out_specs=pl.BlockSpec((tm, d), lambda i, j: (i, 0)),
        out_shape=jax.ShapeDtypeStruct((t, d), F32),
        scratch_shapes=[pltpu.VMEM((tm, d), BF16), pltpu.VMEM((tm, d), F32)],
        compiler_params=_cparams(("parallel", "arbitrary")),
        name="ffn",
    )(x2, mod3, mod3, mod3, norm_w.reshape(1, d), wg, wu, wd, fw.reshape(1, d))


def _shift_in_segment(x4, off):
    nseg, _, sub, c = x4.shape
    r = pltpu.roll(x4, (-off) % sub, axis=2)
    zero = jnp.zeros((nseg, 1, sub, c), x4.dtype)
    row = lax.broadcasted_iota(jnp.int32, (1, 1, sub, c), 2)
    if off < 0:
        return jnp.where(row < -off, jnp.concatenate([zero, r[:, :-1]], axis=1), r)
    return jnp.where(row >= sub - off, jnp.concatenate([r[:, 1:], zero], axis=1), r)


def _inproj_kernel(x_ref, sh_ref, sc_ref, nw_ref, w_ref, wdt_ref, dtb_ref, cw_ref, cb_ref,
                   p_ref, dt_ref, h_scr, acc_scr, *, seg, j0, j_gelu, j_plain, j_conv):
    j = pl.program_id(1) + j0
    tm, tn = p_ref.shape

    @pl.when(j == j0)
    def _():
        _rms_mod_store(x_ref, h_scr, nw_ref[...], sc_ref[...], sh_ref[...])
        dt_ref[...] = jax.nn.softplus(_dot(h_scr[...], wdt_ref[...]) + dtb_ref[...])

    def run(epilogue):
        assert sum(PROJ_CHUNKS) == tn
        starts = [sum(PROJ_CHUNKS[:k]) for k in range(len(PROJ_CHUNKS))]
        cols = [slice(a, a + w) for a, w in zip(starts, PROJ_CHUNKS)]

        def finish(s):
            c = cols[s]
            p_ref[:, c] = epilogue(acc_scr[s % 2, :, :PROJ_CHUNKS[s]], c).astype(BF16)

        for s, c in enumerate(cols):
            if s > 0:
                finish(s - 1)
            acc_scr[s % 2, :, :PROJ_CHUNKS[s]] = _dot(h_scr[...], w_ref[:, c])
        finish(len(cols) - 1)

    @pl.when(j < j_gelu)
    def _():
        run(lambda acc, cols: jax.nn.gelu(acc))

    @pl.when(((j >= j_gelu) & (j < j_plain)) | (j >= j_conv))
    def _():
        run(lambda acc, cols: acc)

    @pl.when((j >= j_plain) & (j < j_conv))
    def _():
        pad = (CONV_K - 1) // 2

        def conv_silu(acc, cols):
            x4 = acc.reshape(tm // seg, seg // SUBLANES, SUBLANES, acc.shape[1])
            y = cb_ref[:, cols] + cw_ref[pad:pad + 1, cols] * x4
            for tap in range(CONV_K):
                if tap != pad:
                    y = y + cw_ref[tap:tap + 1, cols] * _shift_in_segment(x4, tap - pad)
            return jax.nn.silu(y).reshape(acc.shape)

        run(conv_silu)


def _inproj(x2, mod3, mod_row, norm_w, w_all, w_dt, dt_bias, conv_w, conv_b, *, seg, tm, tn,
            n_gelu, n_plain, n_conv, n_gate, tiles=None):
    t, d = x2.shape
    j_gelu = n_gelu // tn
    j_plain = j_gelu + n_plain // tn
    j_conv = j_plain + n_conv // tn
    nconv_t = n_conv // tn
    ngate_t = n_gate // tn
    n_dt = w_dt.shape[1]
    j0, nj = tiles if tiles is not None else (0, j_conv + ngate_t)
    assert t % tm == 0 and tm % seg == 0 and w_all.shape[1] == (j_conv + ngate_t) * tn + n_dt

    def mod_spec(k):
        return pl.BlockSpec((None, 1, d), lambda i, j: (mod_row(i), 0, k))

    def w_col(j):
        return pl.multiple_of(jnp.where(j < j_conv, j * tn, j * tn + n_dt), LANES)

    return pl.pallas_call(
        functools.partial(_inproj_kernel, seg=seg, j0=j0, j_gelu=j_gelu, j_plain=j_plain, j_conv=j_conv),
        grid=(t // tm, nj),
        in_specs=[pl.BlockSpec((tm, d), lambda i, j: (i, 0)),
                  mod_spec(3), mod_spec(4),
                  pl.BlockSpec((1, d), lambda i, j: (0, 0)),
                  pl.BlockSpec((pl.Element(d), pl.Element(tn)), lambda i, j: (0, w_col(j + j0))),
                  pl.BlockSpec((d, LANES), lambda i, j: (0, 0)),
                  pl.BlockSpec((1, LANES), lambda i, j: (0, 0)),
                  pl.BlockSpec((CONV_K, tn), lambda i, j: (0, jnp.clip(j + j0 - j_plain, 0, nconv_t - 1))),
                  pl.BlockSpec((1, tn), lambda i, j: (0, jnp.clip(j + j0 - j_plain, 0, nconv_t - 1)))],
        out_specs=[pl.BlockSpec((tm, tn), lambda i, j: (i, j)),
                   pl.BlockSpec((tm, LANES), lambda i, j: (i, 0))],
        out_shape=[jax.ShapeDtypeStruct((t, nj * tn), BF16), jax.ShapeDtypeStruct((t, LANES), F32)],
        scratch_shapes=[pltpu.VMEM((tm, d), BF16), pltpu.VMEM((2, tm, PROJ_SUB), F32)],
        compiler_params=_cparams(("parallel", "arbitrary")),
        name="inproj",
    )(x2, mod3, mod3, norm_w.reshape(1, d), w_all, w_dt, dt_bias, conv_w, conv_b)


def _gmlp_kernel(u_ref, v_ref, lw_ref, lb_ref, ws_ref, bs_ref, o_ref):
    v = v_ref[...].astype(F32)
    mu = jnp.mean(v, axis=-1, keepdims=True)
    vc = v - mu
    var = jnp.mean(vc * vc, axis=-1, keepdims=True)
    vn = (vc * lax.rsqrt(var + EPS) * lw_ref[...] + lb_ref[...]).astype(BF16)
    tm, width = vn.shape
    gw = width // A_GROUPS
    for c in range(tm // A_CHUNK):
        r0 = c * A_CHUNK
        for g in range(A_GROUPS):
            c0 = g * gw
            s = _dot(ws_ref[g], vn[r0:r0 + A_CHUNK, c0:c0 + gw]) + bs_ref[:, g:g + 1]
            u = u_ref[r0:r0 + A_CHUNK, c0:c0 + gw].astype(F32)
            o_ref[r0:r0 + A_CHUNK, c0:c0 + gw] = (u * s).astype(BF16)


def _gmlp(p, width, ln_w, ln_b, ws, bs_t, *, tm):
    t = p.shape[0]
    return pl.pallas_call(
        _gmlp_kernel,
        grid=(t // tm,),
        in_specs=[pl.BlockSpec((tm, width), lambda i: (i, 0)),
                  pl.BlockSpec((tm, width), lambda i: (i, 1)),
                  pl.BlockSpec((1, width), lambda i: (0, 0)),
                  pl.BlockSpec((1, width), lambda i: (0, 0)),
                  pl.BlockSpec((A_GROUPS, A_CHUNK, A_CHUNK), lambda i: (0, 0, 0)),
                  pl.BlockSpec((A_CHUNK, A_GROUPS), lambda i: (0, 0))],
        out_specs=pl.BlockSpec((tm, width), lambda i: (i, 0)),
        out_shape=jax.ShapeDtypeStruct((t, width), BF16),
        compiler_params=_cparams(("parallel",)),
        name="gmlp",
    )(p, p, ln_w.reshape(1, width), ln_b.reshape(1, width), ws, bs_t)


HPG = 8
GL = 2 * HPG


def _decay_tables(dt, alog_row):
    q = dt.shape[0]
    a = dt * (-jnp.exp(alog_row) * LOG2E)
    ri = lax.broadcasted_iota(jnp.int32, (q, q), 0)
    ci = lax.broadcasted_iota(jnp.int32, (q, q), 1)
    tril = jnp.where(ri >= ci, 1.0, 0.0).astype(BF16)
    triu = jnp.where(ri <= ci, 1.0, 0.0).astype(BF16)
    lane_f = (lax.broadcasted_iota(jnp.int32, (q, LANES), 1) % GL) < HPG
    sub_f = (lax.broadcasted_iota(jnp.int32, (LANES, q), 0) % GL) < HPG
    cs = jnp.where(lane_f, _dot_exact_lhs(tril, a), _dot_exact_lhs(triu, a))
    a_t = a.T
    cst = jnp.where(sub_f, _dot_exact_rhs(a_t, triu), _dot_exact_rhs(a_t, tril))
    return cs, cst - jnp.log2(dt.T)


def _pair_masks(q):
    lane = lax.broadcasted_iota(jnp.int32, (q, LANES), 1)
    return (jnp.where(lane < HEADDIM, 1.0, 0.0).astype(BF16),
            jnp.where(lane >= HEADDIM, 1.0, 0.0).astype(BF16))


def _block_diag_pair(xp, masks):
    return jnp.concatenate([xp * masks[0], xp * masks[1]], axis=0)


def _head_terms(cs_ref, rowt_ref, hl, last):
    q = cs_ref.shape[0]
    col = cs_ref[:, hl:hl + 1]
    rowt = rowt_ref[hl:hl + 1, :]
    clast = jnp.broadcast_to(col[last:last + 1, :], (1, q))
    return col, rowt, clast


def _ctx_state_kernel(x_ref, b_ref, dt_ref, alog_ref, h_ref, cs_scr, rowt_scr):
    q = x_ref.shape[0]
    cs, rowt = _decay_tables(dt_ref[...], alog_ref[...])
    for g in range(GROUPS):
        cs_scr[g] = cs[:, g * GL:(g + 1) * GL]
        rowt_scr[g] = rowt[g * GL:(g + 1) * GL, :]
    masks = _pair_masks(q)

    def group_body(g, carry):
        cs_g, rowt_g = cs_scr[g], rowt_scr[g]
        bg = b_ref[:, pl.ds(pl.multiple_of(g * STATE, STATE), STATE)]
        bg_t = bg.astype(F32).T.astype(BF16)
        for d in range(2):
            last = q - 1 if d == 0 else 0
            for pair in range(HPG // 2):
                col0 = pl.multiple_of(g * (HPG * HEADDIM) + pair * LANES, LANES)
                bd = _block_diag_pair(x_ref[:, pl.ds(col0, LANES)], masks)
                bts = []
                for k in range(2):
                    _, rowt, clast = _head_terms(cs_g, rowt_g, d * HPG + pair * 2 + k, last)
                    bts.append(bg_t * jnp.exp2(clast - rowt).astype(BF16))
                h_ref[d, g, :, pair * LANES:(pair + 1) * LANES] = _dot(jnp.concatenate(bts, axis=1), bd)
        return carry

    lax.fori_loop(0, GROUPS, group_body, 0)


def _ctx_states(pc, dtc, alog_row, *, bsz, q, x_blk, b_blk):
    inner = GROUPS * HPG * HEADDIM
    return pl.pallas_call(
        _ctx_state_kernel,
        grid=(bsz,),
        in_specs=[pl.BlockSpec((q, inner), lambda b: (b, x_blk)),
                  pl.BlockSpec((q, GROUPS * STATE), lambda b: (b, b_blk)),
                  pl.BlockSpec((q, LANES), lambda b: (b, 0)),
                  pl.BlockSpec((1, LANES), lambda b: (0, 0))],
        out_specs=pl.BlockSpec((2, None, GROUPS, STATE, HPG * HEADDIM), lambda b: (0, b, 0, 0, 0)),
        out_shape=jax.ShapeDtypeStruct((2, bsz, GROUPS, STATE, HPG * HEADDIM), F32),
        scratch_shapes=[pltpu.VMEM((GROUPS, q, GL), F32), pltpu.VMEM((GROUPS, GL, q), F32)],
        compiler_params=_cparams(("arbitrary",)),
        name="ctx_state",
    )(pc, pc, dtc, alog_row)


def _ssd_kernel(xf_ref, bf_ref, cf_ref, dtf_ref, xb_ref, bb_ref, cb_ref, dtb_ref, alog_ref, dskip_ref, h0_ref,
                yf_ref, yb_ref, s_scr, cs_scr, rowt_scr):
    q = xf_ref.shape[0]

    @pl.when(pl.program_id(1) == 0)
    def _():
        s_scr[...] = h0_ref[...]

    lane_f = (lax.broadcasted_iota(jnp.int32, (q, LANES), 1) % GL) < HPG
    dt = jnp.where(lane_f, dtf_ref[...], dtb_ref[...])
    cs_scr[...], rowt_scr[...] = _decay_tables(dt, alog_ref[...])

    ri = lax.broadcasted_iota(jnp.int32, (q, q), 0)
    ci = lax.broadcasted_iota(jnp.int32, (q, q), 1)
    lane_lo = lax.broadcasted_iota(jnp.int32, (q, LANES), 1) < HEADDIM
    lane_lo_row = lax.broadcasted_iota(jnp.int32, (1, LANES), 1) < HEADDIM
    masks = _pair_masks(q)
    dirs = ((xf_ref, bf_ref, cf_ref, yf_ref), (xb_ref, bb_ref, cb_ref, yb_ref))

    def group_body(g):
        for d, (x_ref, b_ref, c_ref, y_ref) in enumerate(dirs):
            last = q - 1 if d == 0 else 0
            mask = (ri >= ci) if d == 0 else (ri <= ci)
            gcol = slice(g * STATE, (g + 1) * STATE)
            bg = b_ref[:, gcol]
            cg = c_ref[:, gcol]
            cb = lax.dot_general(cg, bg, (((1,), (1,)), ((), ())), preferred_element_type=F32)
            bg_t = bg.astype(F32).T.astype(BF16)
            for pair in range(HPG // 2):
                psl = slice(pair * LANES, (pair + 1) * LANES)
                s_pair = s_scr[d, g, :, psl]
                y_off = _dot(cg, s_pair.astype(BF16))
                col0 = g * (HPG * HEADDIM) + pair * LANES
                xp = x_ref[:, pl.ds(col0, LANES)]
                bd = _block_diag_pair(xp, masks)
                ms, bts, colbs, elasts = [], [], [], []
                for k in range(2):
                    hl = g * GL + d * HPG + pair * 2 + k
                    col, rowt, clast = _head_terms(cs_scr, rowt_scr, hl, last)
                    colb = jnp.broadcast_to(col, (q, q))
                    ms.append((cb * jnp.exp2(jnp.where(mask, colb - rowt, -jnp.inf))).astype(BF16))
                    bts.append(bg_t * jnp.exp2(clast - rowt).astype(BF16))
                    colbs.append(colb)
                    elasts.append(jnp.exp2(clast))
                y = _dot(jnp.concatenate(ms, axis=1), bd)
                y = y + y_off * jnp.exp2(jnp.where(lane_lo, colbs[0], colbs[1]))
                if d == 0:
                    y = y + xp.astype(F32) * dskip_ref[:, pl.ds(col0, LANES)]
                y_ref[:, pl.ds(col0, LANES)] = y.astype(BF16)
                grow = jnp.where(lane_lo_row, elasts[0], elasts[1])
                s_scr[d, g, :, psl] = s_pair * grow + _dot(jnp.concatenate(bts, axis=1), bd)

    for g in range(GROUPS):
        group_body(g)


def _ssd(p, dt, alog_row, dskip_row, h0, *, bsz, nc, x_blk, b_blk, c_blk):
    q = SSD_CHUNK
    t = p.shape[0]
    inner = GROUPS * HPG * HEADDIM
    gs = GROUPS * STATE

    def fwd(b, c):
        return b * nc + c

    def bwd(b, c):
        return b * nc + (nc - 1 - c)

    def specs(row):
        return [pl.BlockSpec((q, inner), lambda b, c: (row(b, c), x_blk)),
                pl.BlockSpec((q, gs), lambda b, c: (row(b, c), b_blk)),
                pl.BlockSpec((q, gs), lambda b, c: (row(b, c), c_blk)),
                pl.BlockSpec((q, LANES), lambda b, c: (row(b, c), 0))]

    return pl.pallas_call(
        _ssd_kernel,
        grid=(bsz, nc),
        in_specs=specs(fwd) + specs(bwd) + [
            pl.BlockSpec((1, LANES), lambda b, c: (0, 0)),
            pl.BlockSpec((1, inner), lambda b, c: (0, 0)),
            pl.BlockSpec((2, None, GROUPS, STATE, HPG * HEADDIM), lambda b, c: (0, b, 0, 0, 0))],
        out_specs=[pl.BlockSpec((q, inner), lambda b, c: (fwd(b, c), 0)),
                   pl.BlockSpec((q, inner), lambda b, c: (bwd(b, c), 0))],
        out_shape=[jax.ShapeDtypeStruct((t, inner), BF16), jax.ShapeDtypeStruct((t, inner), BF16)],
        scratch_shapes=[pltpu.VMEM((2, GROUPS, STATE, HPG * HEADDIM), F32),
                        pltpu.VMEM((q, LANES), F32), pltpu.VMEM((LANES, q), F32)],
        compiler_params=_cparams(("arbitrary", "arbitrary")),
        name="ssd",
    )(p, p, p, dt, p, p, p, dt, alog_row, dskip_row, h0)


def _gnorm_kernel(yf_ref, yb_ref, z_ref, w_ref, o_ref):
    y = (yf_ref[...].astype(F32) + yb_ref[...].astype(F32)) * jax.nn.silu(z_ref[...].astype(F32))
    gw = y.shape[1] // GROUPS
    for g in range(GROUPS):
        yg = y[:, g * gw:(g + 1) * gw]
        ms = jnp.mean(yg * yg, axis=-1, keepdims=True)
        o_ref[:, g * gw:(g + 1) * gw] = (yg * lax.rsqrt(ms + EPS) * w_ref[:, g * gw:(g + 1) * gw]).astype(BF16)


def _gnorm(yf, yb, p, z_blk, norm_w, *, tm):
    t, inner = yf.shape
    return pl.pallas_call(
        _gnorm_kernel,
        grid=(t // tm,),
        in_specs=[pl.BlockSpec((tm, inner), lambda i: (i, 0)),
                  pl.BlockSpec((tm, inner), lambda i: (i, 0)),
                  pl.BlockSpec((tm, inner), lambda i: (i, z_blk)),
                  pl.BlockSpec((1, inner), lambda i: (0, 0))],
        out_specs=pl.BlockSpec((tm, inner), lambda i: (i, 0)),
        out_shape=jax.ShapeDtypeStruct((t, inner), BF16),
        compiler_params=_cparams(("parallel",)),
        name="gnorm",
    )(yf, yb, p, norm_w.reshape(1, inner))


def _merge_kernel(us_ref, yn_ref, ga_ref, gb_ref, ba_ref, bb_ref, wa_ref, wb_ref, o_ref):
    for s in range(o_ref.shape[1] // SUB):
        cols = slice(s * SUB, (s + 1) * SUB)
        ya = _dot(us_ref[...], wa_ref[:, cols])
        yb = _dot(yn_ref[...], wb_ref[:, cols])
        ga = jax.nn.sigmoid(ga_ref[:, cols].astype(F32) + ba_ref[:, cols])
        gb = jax.nn.sigmoid(gb_ref[:, cols].astype(F32) + bb_ref[:, cols])
        o_ref[:, cols] = (ga * ya + gb * yb).astype(BF16)


def _merge(us, yn, p, gate_col0, b_gate, wa, wb, *, tm, tn):
    t = us.shape[0]
    wa_in, d = wa.shape
    wb_in = wb.shape[0]
    ga_blk = gate_col0 // tn
    gb_blk = (gate_col0 + d) // tn
    return pl.pallas_call(
        _merge_kernel,
        grid=(t // tm, d // tn),
        in_specs=[pl.BlockSpec((tm, wa_in), lambda i, j: (i, 0)),
                  pl.BlockSpec((tm, wb_in), lambda i, j: (i, 0)),
                  pl.BlockSpec((tm, tn), lambda i, j: (i, ga_blk + j)),
                  pl.BlockSpec((tm, tn), lambda i, j: (i, gb_blk + j)),
                  pl.BlockSpec((1, tn), lambda i, j: (0, j)),
                  pl.BlockSpec((1, tn), lambda i, j: (0, d // tn + j)),
                  pl.BlockSpec((wa_in, tn), lambda i, j: (0, j)),
                  pl.BlockSpec((wb_in, tn), lambda i, j: (0, j))],
        out_specs=pl.BlockSpec((tm, tn), lambda i, j: (i, j)),
        out_shape=jax.ShapeDtypeStruct((t, d), BF16),
        compiler_params=_cparams(("parallel", "arbitrary")),
        name="merge",
    )(us, yn, p, p, b_gate, b_gate, wa, wb)


def _outproj_kernel(mg_ref, x_ref, m_ref, wo_ref, o_ref):
    for s in range(o_ref.shape[1] // OUT_SUB):
        cols = slice(s * OUT_SUB, (s + 1) * OUT_SUB)
        o_ref[:, cols] = x_ref[:, cols] + m_ref[:, cols] * _dot(mg_ref[...], wo_ref[:, cols])


def _outproj(mg, x2, mod3, mod_row, wo, *, tm):
    t, d = x2.shape
    return pl.pallas_call(
        _outproj_kernel,
        grid=(t // tm,),
        in_specs=[pl.BlockSpec((tm, d), lambda i: (i, 0)),
                  pl.BlockSpec((tm, d), lambda i: (i, 0)),
                  pl.BlockSpec((None, 1, d), lambda i: (mod_row(i), 0, 5)),
                  pl.BlockSpec((d, d), lambda i: (0, 0))],
        out_specs=pl.BlockSpec((tm, d), lambda i: (i, 0)),
        out_shape=jax.ShapeDtypeStruct((t, d), F32),
        compiler_params=_cparams(("parallel",)),
        name="outproj",
    )(mg, x2, mod3, wo)


def _dt_perm(a):
    lead = a.shape[:-2]
    a = a.reshape(*lead, 2, GROUPS, HPG)
    a = jnp.swapaxes(a, -3, -2)
    return a.reshape(*lead, 2 * GROUPS * HPG)


def kernel(x, c, ctx, c_ctx, w_mod, b_mod, norm_ffn1, ffn1_gate, ffn1_up, ffn1_down, norm_mix, w_in, b_gate,
           gmlp_ln_w, gmlp_ln_b, gmlp_ws, gmlp_bs, w_a, conv_w, conv_b, a_log, dt_bias, d_skip, ssm_norm,
           w_b, w_out, norm_ffn2, ffn2_gate, ffn2_up, ffn2_down, norm_final):
    bsz, seq, d = x.shape
    ctx_len = ctx.shape[1]
    depth = w_mod.shape[0]
    assert depth == 1, "context stream update between layers is not implemented"
    assert bsz < 8
    inner = GROUPS * HPG * HEADDIM
    gs = GROUPS * STATE
    a_width = w_a.shape[1]
    n_heads = GROUPS * HPG
    assert w_b.shape[1] == inner and conv_w.shape[2] == inner + 2 * gs
    assert w_in.shape[2] == 2 * a_width + 2 * inner + 2 * gs + 2 * n_heads + 2 * d

    t = bsz * seq
    tc = bsz * ctx_len
    x2 = x.reshape(t, d)
    ctx2 = ctx.reshape(tc, d)

    off_dt = 2 * a_width + 2 * inner + 2 * gs
    off_gate = off_dt + 2 * n_heads
    n_gelu, n_plain, n_conv = 2 * a_width, inner, inner + 2 * gs
    col_z = n_gelu
    col_x = n_gelu + n_plain
    col_b = col_x + inner
    col_c = col_b + gs
    col_gate = n_gelu + n_plain + n_conv

    i = 0
    w_all = w_in[i].astype(BF16)
    w_dt = _dt_perm(w_in[i][:, off_dt:off_gate].reshape(d, 2, n_heads)).astype(BF16)
    dt_bias_row = _dt_perm(dt_bias[i]).reshape(1, 2 * n_heads)
    alog_row = _dt_perm(a_log[i].astype(F32)).reshape(1, 2 * n_heads)
    dskip_row = jnp.repeat(d_skip[i], HEADDIM).reshape(1, inner)
    bf = lambda w: w.astype(BF16)

    c8 = jnp.zeros((8, d), F32).at[:bsz].set(c).at[bsz].set(c_ctx)
    mod3 = _adaln(c8, w_mod[i], b_mod[i]).reshape(8, 1, N_MOD * d)

    tm = 512
    tm_proj = 512

    def lat_row(rows):
        assert seq % rows == 0, "a token tile must not straddle two samples"
        return lambda ti: (ti * rows) // seq

    ctx_row = lambda ti: bsz

    ffn1_w = (bf(ffn1_gate[i]), bf(ffn1_up[i]), bf(ffn1_down[i]))
    x1 = _ffn(x2, mod3, 0, lat_row(tm), norm_ffn1[i], *ffn1_w, None, tm=tm, tf=512)
    ctx1 = _ffn(ctx2, mod3, 0, ctx_row, norm_ffn1[i], *ffn1_w, None, tm=tm, tf=512)

    tn_proj = 2048
    proj_args = (norm_mix[i], w_all, w_dt, dt_bias_row, conv_w[i], conv_b[i].reshape(1, -1))
    proj_kw = dict(tm=tm_proj, tn=tn_proj, n_gelu=n_gelu, n_plain=n_plain, n_conv=n_conv, n_gate=2 * d)
    ctx_tiles = (col_x // tn_proj, n_conv // tn_proj)
    pc, dtc = _inproj(ctx1, mod3, ctx_row, *proj_args, seg=ctx_len, tiles=ctx_tiles, **proj_kw)
    h0 = _ctx_states(pc, dtc, alog_row, bsz=bsz, q=ctx_len, x_blk=0, b_blk=inner // gs)

    p, dt = _inproj(x1, mod3, lat_row(tm_proj), *proj_args, seg=GRID_W, **proj_kw)
    us = _gmlp(p, a_width, gmlp_ln_w[i], gmlp_ln_b[i], bf(gmlp_ws[i]), gmlp_bs[i].T, tm=tm)
    yf, yb = _ssd(p, dt, alog_row, dskip_row, h0, bsz=bsz, nc=seq // SSD_CHUNK,
                  x_blk=col_x // inner, b_blk=col_b // gs, c_blk=col_c // gs)
    yn = _gnorm(yf, yb, p, col_z // inner, ssm_norm[i], tm=256)
    mg = _merge(us, yn, p, col_gate, b_gate[i].reshape(1, -1), bf(w_a[i]), bf(w_b[i]), tm=1024, tn=512)
    x3 = _outproj(mg, x1, mod3, lat_row(tm), bf(w_out[i]), tm=tm)

    out = _ffn(x3, mod3, 6, lat_row(tm), norm_ffn2[i], bf(ffn2_gate[i]), bf(ffn2_up[i]), bf(ffn2_down[i]),
               norm_final, tm=tm, tf=512)
    return out.reshape(bsz, seq, d)
```

```python
import functools

import jax
import jax.numpy as jnp
from jax import lax
from jax.experimental import pallas as pl
from jax.experimental.pallas import tpu as pltpu

F32 = jnp.float32
BF16 = jnp.bfloat16
EPS = 1e-6
LOG2E = 1.4426950408889634

GRID_W = 64
A_GROUPS = 8
A_CHUNK = 128
HEADDIM = 64
GROUPS = 8
STATE = 128
CONV_K = 5
SSD_CHUNK = 128
N_MOD = 9

LANES = 128
SUB = 256
OUT_SUB = 512
PROJ_SUB = 512
PROJ_CHUNKS = (256, 512, 256, 512, 256, 256)
SUBLANES = 8
BF16_ROWS = 16
VMEM_LIMIT = 56 * 1024 * 1024


def _cparams(sem):
    return pltpu.CompilerParams(dimension_semantics=sem, vmem_limit_bytes=VMEM_LIMIT)


def _dot(a, b):
    return jnp.dot(a, b, preferred_element_type=F32)


def _split3(a):
    a1 = a.astype(BF16)
    r1 = a - a1.astype(F32)
    a2 = r1.astype(BF16)
    r2 = r1 - a2.astype(F32)
    return a1, a2, r2.astype(BF16)


def _dot_exact_lhs(tri, a):
    a1, a2, a3 = _split3(a)
    return _dot(tri, a1) + _dot(tri, a2) + _dot(tri, a3)


def _dot_exact_rhs(a, tri):
    a1, a2, a3 = _split3(a)
    return _dot(a1, tri) + _dot(a2, tri) + _dot(a3, tri)


def _rms_mod_store(x_ref, h_ref, nw, scale, shift):
    gain = nw * (1.0 + scale)
    for r in range(0, x_ref.shape[0], BF16_ROWS):
        x = x_ref[r:r + BF16_ROWS, :]
        ms = jnp.mean(x * x, axis=-1, keepdims=True)
        h_ref[r:r + BF16_ROWS, :] = (x * lax.rsqrt(ms + EPS) * gain + shift).astype(BF16)


def _adaln_kernel(c_ref, w_ref, b_ref, o_ref):
    s = jax.nn.silu(c_ref[...])
    s1, s2, s3 = _split3(s)
    w = w_ref[...].astype(BF16)
    o_ref[...] = _dot(s1, w) + _dot(s2, w) + _dot(s3, w) + b_ref[...]


def _adaln(c8, w_mod, b_mod):
    d, n = w_mod.shape
    tn = 1024
    return pl.pallas_call(
        _adaln_kernel,
        grid=(n // tn,),
        in_specs=[pl.BlockSpec((8, d), lambda j: (0, 0)),
                  pl.BlockSpec((d, tn), lambda j: (0, j)),
                  pl.BlockSpec((1, tn), lambda j: (0, j))],
        out_specs=pl.BlockSpec((8, tn), lambda j: (0, j)),
        out_shape=jax.ShapeDtypeStruct((8, n), F32),
        compiler_params=_cparams(("arbitrary",)),
        name="adaln",
    )(c8, w_mod, b_mod.reshape(1, n))


def _ffn_kernel(x_ref, sh_ref, sc_ref, gt_ref, nw_ref, wg_ref, wu_ref, wd_ref, fw_ref, o_ref,
                h_scr, acc_scr, *, final_norm):
    j = pl.program_id(1)

    @pl.when(j == 0)
    def _():
        _rms_mod_store(x_ref, h_scr, nw_ref[...], sc_ref[...], sh_ref[...])
        acc_scr[...] = jnp.zeros_like(acc_scr)

    acts = []
    for s in range(wg_ref.shape[1] // SUB):
        cols = slice(s * SUB, (s + 1) * SUB)
        g = _dot(h_scr[...], wg_ref[:, cols])
        u = _dot(h_scr[...], wu_ref[:, cols])
        acts.append((jax.nn.silu(g) * u).astype(BF16))
    acc_scr[...] += _dot(jnp.concatenate(acts, axis=1), wd_ref[...])

    @pl.when(j == pl.num_programs(1) - 1)
    def _():
        half_gate = 0.5 * gt_ref[...]
        for r in range(0, x_ref.shape[0], SUBLANES):
            rows = slice(r, r + SUBLANES)
            out = x_ref[rows, :] + half_gate * acc_scr[rows, :]
            if final_norm:
                ms = jnp.mean(out * out, axis=-1, keepdims=True)
                out = out * lax.rsqrt(ms + EPS) * fw_ref[...]
            o_ref[rows, :] = out


def _ffn(x2, mod3, mod_k, mod_row, norm_w, wg, wu, wd, final_w, *, tm, tf):
    t, d = x2.shape
    f = wg.shape[1]
    final_norm = final_w is not None
    fw = final_w if final_norm else norm_w
    assert t % tm == 0 and f % tf == 0

    def mod_spec(k):
        return pl.BlockSpec((None, 1, d), lambda i, j: (mod_row(i), 0, k))

    return pl.pallas_call(
        functools.partial(_ffn_kernel, final_norm=final_norm),
        grid=(t // tm, f // tf),
        in_specs=[pl.BlockSpec((tm, d), lambda i, j: (i, 0)),
                  mod_spec(mod_k), mod_spec(mod_k + 1), mod_spec(mod_k + 2),
                  pl.BlockSpec((1, d), lambda i, j: (0, 0)),
                  pl.BlockSpec((d, tf), lambda i, j: (0, j)),
                  pl.BlockSpec((d, tf), lambda i, j: (0, j)),
                  pl.BlockSpec((tf, d), lambda i, j: (j, 0)),
                  pl.BlockSpec((1, d), lambda i, j: (0, 0))],
        out_specs=pl.BlockSpec((tm, d), lambda i, j: (i, 0)),
        out_shape=jax.ShapeDtypeStruct((t, d), F32),
        scratch_shapes=[pltpu.VMEM((tm, d), BF16), pltpu.VMEM((tm, d), F32)],
        compiler_params=_cparams(("parallel", "arbitrary")),
        name="ffn",
    )(x2, mod3, mod3, mod3, norm_w.reshape(1, d), wg, wu, wd, fw.reshape(1, d))


def _shift_in_segment(x4, off):
    nseg, _, sub, c = x4.shape
    r = pltpu.roll(x4, (-off) % sub, axis=2)
    zero = jnp.zeros((nseg, 1, sub, c), x4.dtype)
    row = lax.broadcasted_iota(jnp.int32, (1, 1, sub, c), 2)
    if off < 0:
        return jnp.where(row < -off, jnp.concatenate([zero, r[:, :-1]], axis=1), r)
    return jnp.where(row >= sub - off, jnp.concatenate([r[:, 1:], zero], axis=1), r)


def _inproj_kernel(x_ref, sh_ref, sc_ref, nw_ref, w_ref, wdt_ref, dtb_ref, cw_ref, cb_ref,
                   p_ref, dt_ref, h_scr, acc_scr, *, seg, j0, j_gelu, j_plain, j_conv):
    j = pl.program_id(1) + j0
    tm, tn = p_ref.shape

    @pl.when(j == j0)
    def _():
        _rms_mod_store(x_ref, h_scr, nw_ref[...], sc_ref[...], sh_ref[...])
        dt_ref[...] = jax.nn.softplus(_dot(h_scr[...], wdt_ref[...]) + dtb_ref[...])

    def run(epilogue):
        assert sum(PROJ_CHUNKS) == tn
        starts = [sum(PROJ_CHUNKS[:k]) for k in range(len(PROJ_CHUNKS))]
        cols = [slice(a, a + w) for a, w in zip(starts, PROJ_CHUNKS)]

        def finish(s):
            c = cols[s]
            p_ref[:, c] = epilogue(acc_scr[s % 2, :, :PROJ_CHUNKS[s]], c).astype(BF16)

        for s, c in enumerate(cols):
            if s > 0:
                finish(s - 1)
            acc_scr[s % 2, :, :PROJ_CHUNKS[s]] = _dot(h_scr[...], w_ref[:, c])
        finish(len(cols) - 1)

    @pl.when(j < j_gelu)
    def _():
        run(lambda acc, cols: jax.nn.gelu(acc))

    @pl.when(((j >= j_gelu) & (j < j_plain)) | (j >= j_conv))
    def _():
        run(lambda acc, cols: acc)

    @pl.when((j >= j_plain) & (j < j_conv))
    def _():
        pad = (CONV_K - 1) // 2

        def conv_silu(acc, cols):
            x4 = acc.reshape(tm // seg, seg // SUBLANES, SUBLANES, acc.shape[1])
            y = cb_ref[:, cols] + cw_ref[pad:pad + 1, cols] * x4
            for tap in range(CONV_K):
                if tap != pad:
                    y = y + cw_ref[tap:tap + 1, cols] * _shift_in_segment(x4, tap - pad)
            return jax.nn.silu(y).reshape(acc.shape)

        run(conv_silu)


def _inproj(x2, mod3, mod_row, norm_w, w_all, w_dt, dt_bias, conv_w, conv_b, *, seg, tm, tn,
            n_gelu, n_plain, n_conv, n_gate, tiles=None):
    t, d = x2.shape
    j_gelu = n_gelu // tn
    j_plain = j_gelu + n_plain // tn
    j_conv = j_plain + n_conv // tn
    nconv_t = n_conv // tn
    ngate_t = n_gate // tn
    n_dt = w_dt.shape[1]
    j0, nj = tiles if tiles is not None else (0, j_conv + ngate_t)
    assert t % tm == 0 and tm % seg == 0 and w_all.shape[1] == (j_conv + ngate_t) * tn + n_dt

    def mod_spec(k):
        return pl.BlockSpec((None, 1, d), lambda i, j: (mod_row(i), 0, k))

    def w_col(j):
        return pl.multiple_of(jnp.where(j < j_conv, j * tn, j * tn + n_dt), LANES)

    return pl.pallas_call(
        functools.partial(_inproj_kernel, seg=seg, j0=j0, j_gelu=j_gelu, j_plain=j_plain, j_conv=j_conv),
        grid=(t // tm, nj),
        in_specs=[pl.BlockSpec((tm, d), lambda i, j: (i, 0)),
                  mod_spec(3), mod_spec(4),
                  pl.BlockSpec((1, d), lambda i, j: (0, 0)),
                  pl.BlockSpec((pl.Element(d), pl.Element(tn)), lambda i, j: (0, w_col(j + j0))),
                  pl.BlockSpec((d, LANES), lambda i, j: (0, 0)),
                  pl.BlockSpec((1, LANES), lambda i, j: (0, 0)),
                  pl.BlockSpec((CONV_K, tn), lambda i, j: (0, jnp.clip(j + j0 - j_plain, 0, nconv_t - 1))),
                  pl.BlockSpec((1, tn), lambda i, j: (0, jnp.clip(j + j0 - j_plain, 0, nconv_t - 1)))],
        out_specs=[pl.BlockSpec((tm, tn), lambda i, j: (i, j)),
                   pl.BlockSpec((tm, LANES), lambda i, j: (i, 0))],
        out_shape=[jax.ShapeDtypeStruct((t, nj * tn), BF16), jax.ShapeDtypeStruct((t, LANES), F32)],
        scratch_shapes=[pltpu.VMEM((tm, d), BF16), pltpu.VMEM((2, tm, PROJ_SUB), F32)],
        compiler_params=_cparams(("parallel", "arbitrary")),
        name="inproj",
    )(x2, mod3, mod3, norm_w.reshape(1, d), w_all, w_dt, dt_bias, conv_w, conv_b)


def _gmlp_kernel(u_ref, v_ref, lw_ref, lb_ref, ws_ref, bs_ref, o_ref):
    v = v_ref[...].astype(F32)
    mu = jnp.mean(v, axis=-1, keepdims=True)
    vc = v - mu
    var = jnp.mean(vc * vc, axis=-1, keepdims=True)
    vn = (vc * lax.rsqrt(var + EPS) * lw_ref[...] + lb_ref[...]).astype(BF16)
    tm, width = vn.shape
    gw = width // A_GROUPS
    for c in range(tm // A_CHUNK):
        r0 = c * A_CHUNK
        for g in range(A_GROUPS):
            c0 = g * gw
            s = _dot(ws_ref[g], vn[r0:r0 + A_CHUNK, c0:c0 + gw]) + bs_ref[:, g:g + 1]
            u = u_ref[r0:r0 + A_CHUNK, c0:c0 + gw].astype(F32)
            o_ref[r0:r0 + A_CHUNK, c0:c0 + gw] = (u * s).astype(BF16)


def _gmlp(p, width, ln_w, ln_b, ws, bs_t, *, tm):
    t = p.shape[0]
    return pl.pallas_call(
        _gmlp_kernel,
        grid=(t // tm,),
        in_specs=[pl.BlockSpec((tm, width), lambda i: (i, 0)),
                  pl.BlockSpec((tm, width), lambda i: (i, 1)),
                  pl.BlockSpec((1, width), lambda i: (0, 0)),
                  pl.BlockSpec((1, width), lambda i: (0, 0)),
                  pl.BlockSpec((A_GROUPS, A_CHUNK, A_CHUNK), lambda i: (0, 0, 0)),
                  pl.BlockSpec((A_CHUNK, A_GROUPS), lambda i: (0, 0))],
        out_specs=pl.BlockSpec((tm, width), lambda i: (i, 0)),
        out_shape=jax.ShapeDtypeStruct((t, width), BF16),
        compiler_params=_cparams(("parallel",)),
        name="gmlp",
    )(p, p, ln_w.reshape(1, width), ln_b.reshape(1, width), ws, bs_t)


HPG = 8
GL = 2 * HPG


def _decay_tables(dt, alog_row):
    q = dt.shape[0]
    a = dt * (-jnp.exp(alog_row) * LOG2E)
    ri = lax.broadcasted_iota(jnp.int32, (q, q), 0)
    ci = lax.broadcasted_iota(jnp.int32, (q, q), 1)
    tril = jnp.where(ri >= ci, 1.0, 0.0).astype(BF16)
    triu = jnp.where(ri <= ci, 1.0, 0.0).astype(BF16)
    lane_f = (lax.broadcasted_iota(jnp.int32, (q, LANES), 1) % GL) < HPG
    sub_f = (lax.broadcasted_iota(jnp.int32, (LANES, q), 0) % GL) < HPG
    cs = jnp.where(lane_f, _dot_exact_lhs(tril, a), _dot_exact_lhs(triu, a))
    a_t = a.T
    cst = jnp.where(sub_f, _dot_exact_rhs(a_t, triu), _dot_exact_rhs(a_t, tril))
    return cs, cst - jnp.log2(dt.T)


def _pair_masks(q):
    lane = lax.broadcasted_iota(jnp.int32, (q, LANES), 1)
    return (jnp.where(lane < HEADDIM, 1.0, 0.0).astype(BF16),
            jnp.where(lane >= HEADDIM, 1.0, 0.0).astype(BF16))


def _block_diag_pair(xp, masks):
    return jnp.concatenate([xp * masks[0], xp * masks[1]], axis=0)


def _head_terms(cs_ref, rowt_ref, hl, last):
    q = cs_ref.shape[0]
    col = cs_ref[:, hl:hl + 1]
    rowt = rowt_ref[hl:hl + 1, :]
    clast = jnp.broadcast_to(col[last:last + 1, :], (1, q))
    return col, rowt, clast


def _ctx_state_kernel(x_ref, b_ref, dt_ref, alog_ref, h_ref, cs_scr, rowt_scr):
    q = x_ref.shape[0]
    cs, rowt = _decay_tables(dt_ref[...], alog_ref[...])
    for g in range(GROUPS):
        cs_scr[g] = cs[:, g * GL:(g + 1) * GL]
        rowt_scr[g] = rowt[g * GL:(g + 1) * GL, :]
    masks = _pair_masks(q)

    def group_body(g, carry):
        cs_g, rowt_g = cs_scr[g], rowt_scr[g]
        bg = b_ref[:, pl.ds(pl.multiple_of(g * STATE, STATE), STATE)]
        bg_t = bg.astype(F32).T.astype(BF16)
        for d in range(2):
            last = q - 1 if d == 0 else 0
            for pair in range(HPG // 2):
                col0 = pl.multiple_of(g * (HPG * HEADDIM) + pair * LANES, LANES)
                bd = _block_diag_pair(x_ref[:, pl.ds(col0, LANES)], masks)
                bts = []
                for k in range(2):
                    _, rowt, clast = _head_terms(cs_g, rowt_g, d * HPG + pair * 2 + k, last)
                    bts.append(bg_t * jnp.exp2(clast - rowt).astype(BF16))
                h_ref[d, g, :, pair * LANES:(pair + 1) * LANES] = _dot(jnp.concatenate(bts, axis=1), bd)
        return carry

    lax.fori_loop(0, GROUPS, group_body, 0)


def _ctx_states(pc, dtc, alog_row, *, bsz, q, x_blk, b_blk):
    inner = GROUPS * HPG * HEADDIM
    return pl.pallas_call(
        _ctx_state_kernel,
        grid=(bsz,),
        in_specs=[pl.BlockSpec((q, inner), lambda b: (b, x_blk)),
                  pl.BlockSpec((q, GROUPS * STATE), lambda b: (b, b_blk)),
                  pl.BlockSpec((q, LANES), lambda b: (b, 0)),
                  pl.BlockSpec((1, LANES), lambda b: (0, 0))],
        out_specs=pl.BlockSpec((2, None, GROUPS, STATE, HPG * HEADDIM), lambda b: (0, b, 0, 0, 0)),
        out_shape=jax.ShapeDtypeStruct((2, bsz, GROUPS, STATE, HPG * HEADDIM), F32),
        scratch_shapes=[pltpu.VMEM((GROUPS, q, GL), F32), pltpu.VMEM((GROUPS, GL, q), F32)],
        compiler_params=_cparams(("arbitrary",)),
        name="ctx_state",
    )(pc, pc, dtc, alog_row)


def _ssd_kernel(xf_ref, bf_ref, cf_ref, dtf_ref, xb_ref, bb_ref, cb_ref, dtb_ref, alog_ref, dskip_ref, h0_ref,
                yf_ref, yb_ref, s_scr, cs_scr, rowt_scr):
    q = xf_ref.shape[0]

    @pl.when(pl.program_id(1) == 0)
    def _():
        s_scr[...] = h0_ref[...]

    lane_f = (lax.broadcasted_iota(jnp.int32, (q, LANES), 1) % GL) < HPG
    dt = jnp.where(lane_f, dtf_ref[...], dtb_ref[...])
    cs_scr[...], rowt_scr[...] = _decay_tables(dt, alog_ref[...])

    ri = lax.broadcasted_iota(jnp.int32, (q, q), 0)
    ci = lax.broadcasted_iota(jnp.int32, (q, q), 1)
    lane_lo = lax.broadcasted_iota(jnp.int32, (q, LANES), 1) < HEADDIM
    lane_lo_row = lax.broadcasted_iota(jnp.int32, (1, LANES), 1) < HEADDIM
    masks = _pair_masks(q)
    dirs = ((xf_ref, bf_ref, cf_ref, yf_ref), (xb_ref, bb_ref, cb_ref, yb_ref))

    def group_body(g):
        for d, (x_ref, b_ref, c_ref, y_ref) in enumerate(dirs):
            last = q - 1 if d == 0 else 0
            mask = (ri >= ci) if d == 0 else (ri <= ci)
            gcol = slice(g * STATE, (g + 1) * STATE)
            bg = b_ref[:, gcol]
            cg = c_ref[:, gcol]
            cb = lax.dot_general(cg, bg, (((1,), (1,)), ((), ())), preferred_element_type=F32)
            bg_t = bg.astype(F32).T.astype(BF16)
            for pair in range(HPG // 2):
                psl = slice(pair * LANES, (pair + 1) * LANES)
                s_pair = s_scr[d, g, :, psl]
                y_off = _dot(cg, s_pair.astype(BF16))
                col0 = g * (HPG * HEADDIM) + pair * LANES
                xp = x_ref[:, pl.ds(col0, LANES)]
                bd = _block_diag_pair(xp, masks)
                ms, bts, colbs, elasts = [], [], [], []
                for k in range(2):
                    hl = g * GL + d * HPG + pair * 2 + k
                    col, rowt, clast = _head_terms(cs_scr, rowt_scr, hl, last)
                    colb = jnp.broadcast_to(col, (q, q))
                    ms.append((cb * jnp.exp2(jnp.where(mask, colb - rowt, -jnp.inf))).astype(BF16))
                    bts.append(bg_t * jnp.exp2(clast - rowt).astype(BF16))
                    colbs.append(colb)
                    elasts.append(jnp.exp2(clast))
                y = _dot(jnp.concatenate(ms, axis=1), bd)
                y = y + y_off * jnp.exp2(jnp.where(lane_lo, colbs[0], colbs[1]))
                if d == 0:
                    y = y + xp.astype(F32) * dskip_ref[:, pl.ds(col0, LANES)]
                y_ref[:, pl.ds(col0, LANES)] = y.astype(BF16)
                grow = jnp.where(lane_lo_row, elasts[0], elasts[1])
                s_scr[d, g, :, psl] = s_pair * grow + _dot(jnp.concatenate(bts, axis=1), bd)

    for g in range(GROUPS):
        group_body(g)


def _ssd(p, dt, alog_row, dskip_row, h0, *, bsz, nc, x_blk, b_blk, c_blk):
    q = SSD_CHUNK
    t = p.shape[0]
    inner = GROUPS * HPG * HEADDIM
    gs = GROUPS * STATE

    def fwd(b, c):
        return b * nc + c

    def bwd(b, c):
        return b * nc + (nc - 1 - c)

    def specs(row):
        return [pl.BlockSpec((q, inner), lambda b, c: (row(b, c), x_blk)),
                pl.BlockSpec((q, gs), lambda b, c: (row(b, c), b_blk)),
                pl.BlockSpec((q, gs), lambda b, c: (row(b, c), c_blk)),
                pl.BlockSpec((q, LANES), lambda b, c: (row(b, c), 0))]

    return pl.pallas_call(
        _ssd_kernel,
        grid=(bsz, nc),
        in_specs=specs(fwd) + specs(bwd) + [
            pl.BlockSpec((1, LANES), lambda b, c: (0, 0)),
            pl.BlockSpec((1, inner), lambda b, c: (0, 0)),
            pl.BlockSpec((2, None, GROUPS, STATE, HPG * HEADDIM), lambda b, c: (0, b, 0, 0, 0))],
        out_specs=[pl.BlockSpec((q, inner), lambda b, c: (fwd(b, c), 0)),
                   pl.BlockSpec((q, inner), lambda b, c: (bwd(b, c), 0))],
        out_shape=[jax.ShapeDtypeStruct((t, inner), BF16), jax.ShapeDtypeStruct((t, inner), BF16)],
        scratch_shapes=[pltpu.VMEM((2, GROUPS, STATE, HPG * HEADDIM), F32),
                        pltpu.VMEM((q, LANES), F32), pltpu.VMEM((LANES, q), F32)],
        compiler_params=_cparams(("arbitrary", "arbitrary")),
        name="ssd",
    )(p, p, p, dt, p, p, p, dt, alog_row, dskip_row, h0)


def _gnorm_kernel(yf_ref, yb_ref, z_ref, w_ref, o_ref):
    y = (yf_ref[...].astype(F32) + yb_ref[...].astype(F32)) * jax.nn.silu(z_ref[...].astype(F32))
    gw = y.shape[1] // GROUPS
    for g in range(GROUPS):
        yg = y[:, g * gw:(g + 1) * gw]
        ms = jnp.mean(yg * yg, axis=-1, keepdims=True)
        o_ref[:, g * gw:(g + 1) * gw] = (yg * lax.rsqrt(ms + EPS) * w_ref[:, g * gw:(g + 1) * gw]).astype(BF16)


def _gnorm(yf, yb, p, z_blk, norm_w, *, tm):
    t, inner = yf.shape
    return pl.pallas_call(
        _gnorm_kernel,
        grid=(t // tm,),
        in_specs=[pl.BlockSpec((tm, inner), lambda i: (i, 0)),
                  pl.BlockSpec((tm, inner), lambda i: (i, 0)),
                  pl.BlockSpec((tm, inner), lambda i: (i, z_blk)),
                  pl.BlockSpec((1, inner), lambda i: (0, 0))],
        out_specs=pl.BlockSpec((tm, inner), lambda i: (i, 0)),
        out_shape=jax.ShapeDtypeStruct((t, inner), BF16),
        compiler_params=_cparams(("parallel",)),
        name="gnorm",
    )(yf, yb, p, norm_w.reshape(1, inner))


def _merge_kernel(us_ref, yn_ref, ga_ref, gb_ref, ba_ref, bb_ref, wa_ref, wb_ref, o_ref):
    for s in range(o_ref.shape[1] // SUB):
        cols = slice(s * SUB, (s + 1) * SUB)
        ya = _dot(us_ref[...], wa_ref[:, cols])
        yb = _dot(yn_ref[...], wb_ref[:, cols])
        ga = jax.nn.sigmoid(ga_ref[:, cols].astype(F32) + ba_ref[:, cols])
        gb = jax.nn.sigmoid(gb_ref[:, cols].astype(F32) + bb_ref[:, cols])
        o_ref[:, cols] = (ga * ya + gb * yb).astype(BF16)


def _merge(us, yn, p, gate_col0, b_gate, wa, wb, *, tm, tn):
    t = us.shape[0]
    wa_in, d = wa.shape
    wb_in = wb.shape[0]
    ga_blk = gate_col0 // tn
    gb_blk = (gate_col0 + d) // tn
    return pl.pallas_call(
        _merge_kernel,
        grid=(t // tm, d // tn),
        in_specs=[pl.BlockSpec((tm, wa_in), lambda i, j: (i, 0)),
                  pl.BlockSpec((tm, wb_in), lambda i, j: (i, 0)),
                  pl.BlockSpec((tm, tn), lambda i, j: (i, ga_blk + j)),
                  pl.BlockSpec((tm, tn), lambda i, j: (i, gb_blk + j)),
                  pl.BlockSpec((1, tn), lambda i, j: (0, j)),
                  pl.BlockSpec((1, tn), lambda i, j: (0, d // tn + j)),
                  pl.BlockSpec((wa_in, tn), lambda i, j: (0, j)),
                  pl.BlockSpec((wb_in, tn), lambda i, j: (0, j))],
        out_specs=pl.BlockSpec((tm, tn), lambda i, j: (i, j)),
        out_shape=jax.ShapeDtypeStruct((t, d), BF16),
        compiler_params=_cparams(("parallel", "arbitrary")),
        name="merge",
    )(us, yn, p, p, b_gate, b_gate, wa, wb)


def _outproj_kernel(mg_ref, x_ref, m_ref, wo_ref, o_ref):
    for s in range(o_ref.shape[1] // OUT_SUB):
        cols = slice(s * OUT_SUB, (s + 1) * OUT_SUB)
        o_ref[:, cols] = x_ref[:, cols] + m_ref[:, cols] * _dot(mg_ref[...], wo_ref[:, cols])


def _outproj(mg, x2, mod3, mod_row, wo, *, tm):
    t, d = x2.shape
    return pl.pallas_call(
        _outproj_kernel,
        grid=(t // tm,),
        in_specs=[pl.BlockSpec((tm, d), lambda i: (i, 0)),
                  pl.BlockSpec((tm, d), lambda i: (i, 0)),
                  pl.BlockSpec((None, 1, d), lambda i: (mod_row(i), 0, 5)),
                  pl.BlockSpec((d, d), lambda i: (0, 0))],
        out_specs=pl.BlockSpec((tm, d), lambda i: (i, 0)),
        out_shape=jax.ShapeDtypeStruct((t, d), F32),
        compiler_params=_cparams(("parallel",)),
        name="outproj",
    )(mg, x2, mod3, wo)


def _dt_perm(a):
    lead = a.shape[:-2]
    a = a.reshape(*lead, 2, GROUPS, HPG)
    a = jnp.swapaxes(a, -3, -2)
    return a.reshape(*lead, 2 * GROUPS * HPG)


def kernel(x, c, ctx, c_ctx, w_mod, b_mod, norm_ffn1, ffn1_gate, ffn1_up, ffn1_down, norm_mix, w_in, b_gate,
           gmlp_ln_w, gmlp_ln_b, gmlp_ws, gmlp_bs, w_a, conv_w, conv_b, a_log, dt_bias, d_skip, ssm_norm,
           w_b, w_out, norm_ffn2, ffn2_gate, ffn2_up, ffn2_down, norm_final):
    bsz, seq, d = x.shape
    ctx_len = ctx.shape[1]
    depth = w_mod.shape[0]
    assert depth == 1, "context stream update between layers is not implemented"
    assert bsz < 8
    inner = GROUPS * HPG * HEADDIM
    gs = GROUPS * STATE
    a_width = w_a.shape[1]
    n_heads = GROUPS * HPG
    assert w_b.shape[1] == inner and conv_w.shape[2] == inner + 2 * gs
    assert w_in.shape[2] == 2 * a_width + 2 * inner + 2 * gs + 2 * n_heads + 2 * d

    t = bsz * seq
    tc = bsz * ctx_len
    x2 = x.reshape(t, d)
    ctx2 = ctx.reshape(tc, d)

    off_dt = 2 * a_width + 2 * inner + 2 * gs
    off_gate = off_dt + 2 * n_heads
    n_gelu, n_plain, n_conv = 2 * a_width, inner, inner + 2 * gs
    col_z = n_gelu
    col_x = n_gelu + n_plain
    col_b = col_x + inner
    col_c = col_b + gs
    col_gate = n_gelu + n_plain + n_conv

    i = 0
    w_all = w_in[i].astype(BF16)
    w_dt = _dt_perm(w_in[i][:, off_dt:off_gate].reshape(d, 2, n_heads)).astype(BF16)
    dt_bias_row = _dt_perm(dt_bias[i]).reshape(1, 2 * n_heads)
    alog_row = _dt_perm(a_log[i].astype(F32)).reshape(1, 2 * n_heads)
    dskip_row = jnp.repeat(d_skip[i], HEADDIM).reshape(1, inner)
    bf = lambda w: w.astype(BF16)

    c8 = jnp.zeros((8, d), F32).at[:bsz].set(c).at[bsz].set(c_ctx)
    mod3 = _adaln(c8, w_mod[i], b_mod[i]).reshape(8, 1, N_MOD * d)

    tm = 512
    tm_proj = 512

    def lat_row(rows):
        assert seq % rows == 0, "a token tile must not straddle two samples"
        return lambda ti: (ti * rows) // seq

    ctx_row = lambda ti: bsz

    ffn1_w = (bf(ffn1_gate[i]), bf(ffn1_up[i]), bf(ffn1_down[i]))
    x1 = _ffn(x2, mod3, 0, lat_row(tm), norm_ffn1[i], *ffn1_w, None, tm=tm, tf=512)
    ctx1 = _ffn(ctx2, mod3, 0, ctx_row, norm_ffn1[i], *ffn1_w, None, tm=tm, tf=512)

    tn_proj = 2048
    proj_args = (norm_mix[i], w_all, w_dt, dt_bias_row, conv_w[i], conv_b[i].reshape(1, -1))
    proj_kw = dict(tm=tm_proj, tn=tn_proj, n_gelu=n_gelu, n_plain=n_plain, n_conv=n_conv, n_gate=2 * d)
    ctx_tiles = (col_x // tn_proj, n_conv // tn_proj)
    pc, dtc = _inproj(ctx1, mod3, ctx_row, *proj_args, seg=ctx_len, tiles=ctx_tiles, **proj_kw)
    h0 = _ctx_states(pc, dtc, alog_row, bsz=bsz, q=ctx_len, x_blk=0, b_blk=inner // gs)

    p, dt = _inproj(x1, mod3, lat_row(tm_proj), *proj_args, seg=GRID_W, **proj_kw)
    us = _gmlp(p, a_width, gmlp_ln_w[i], gmlp_ln_b[i], bf(gmlp_ws[i]), gmlp_bs[i].T, tm=tm)
    yf, yb = _ssd(p, dt, alog_row, dskip_row, h0, bsz=bsz, nc=seq // SSD_CHUNK,
                  x_blk=col_x // inner, b_blk=col_b // gs, c_blk=col_c // gs)
    yn = _gnorm(yf, yb, p, col_z // inner, ssm_norm[i], tm=256)
    mg = _merge(us, yn, p, col_gate, b_gate[i].reshape(1, -1), bf(w_a[i]), bf(w_b[i]), tm=1024, tn=512)
    x3 = _outproj(mg, x1, mod3, lat_row(tm), bf(w_out[i]), tm=tm)

    out = _ffn(x3, mod3, 6, lat_row(tm), norm_ffn2[i], bf(ffn2_gate[i]), bf(ffn2_up[i]), bf(ffn2_down[i]),
               norm_final, tm=tm, tf=512)
    return out.reshape(bsz, seq, d)
```
